```python
import jax, jax.numpy as jnp
from jax import lax
import numpy as np

D_MODEL = 1024
BATCH = 1
SEQ = 16384
DEPTH = 4

N_A_LAYERS = DEPTH // 2
N_B_LAYERS = DEPTH - N_A_LAYERS
D_FF = 2816
RMS_EPS = 1e-6

RWKV_HEAD_DIM = 64
RWKV_HEADS = D_MODEL // RWKV_HEAD_DIM
D_DECAY_LORA = 64
D_AAA_LORA = 64
D_MV_LORA = 32
D_GATE_LORA = 64
GN_EPS = 64e-5

DIL_GROUPS = ((128, 1), (512, 4), (2048, 16))
N_GROUPS = len(DIL_GROUPS)
ATTN_HEAD_DIM = 64
ATTN_HEADS = D_MODEL // ATTN_HEAD_DIM
Q_WIDTH = N_GROUPS * ATTN_HEADS * ATTN_HEAD_DIM
NEG_INF = -1e30

kernel_name = "rwkv7_yoco_dilated_macaron"


def rms_norm(x, g):
    xf = x.astype(jnp.float32)
    y = xf * lax.rsqrt(jnp.mean(xf * xf, axis=-1, keepdims=True) + RMS_EPS)
    return (y * g.astype(jnp.float32)).astype(x.dtype)


def swiglu(h, w_in, w_out):
    gate, up = jnp.split(h @ w_in, 2, axis=-1)
    return (jax.nn.silu(gate) * up) @ w_out


def token_shift(x):
    return jnp.pad(x, ((0, 0), (1, 0), (0, 0)))[:, :-1]


def rwkv7_recurrence(r, decay, k, v, a, b):
    Bsz, T, H, N = r.shape

    def step(S, inp):
        r_t, w_t, k_t, v_t, a_t, b_t = inp
        sa = jnp.einsum('bhij,bhj->bhi', S, a_t)
        S = S * w_t[:, :, None, :] + sa[..., None] * b_t[:, :, None, :] + v_t[..., None] * k_t[:, :, None, :]
        y = jnp.einsum('bhij,bhj->bhi', S, r_t)
        return S, y

    xs = tuple(jnp.moveaxis(t.astype(jnp.float32), 1, 0) for t in (r, decay, k, v, a, b))
    S0 = jnp.zeros((Bsz, H, N, N), jnp.float32)
    _, ys = lax.scan(step, S0, xs)
    return jnp.moveaxis(ys, 0, 1)


def rwkv7_time_mix(h, v_first, vres, mu, w_rkv, w0, w1, w2, a0, a1, a2, g1, g2,
                   k_k, k_a, r_k, ln_w, ln_b, w_o):
    Bsz, T, C = h.shape
    heads = lambda t: t.reshape(Bsz, T, RWKV_HEADS, RWKV_HEAD_DIM)
    xx = token_shift(h) - h
    xr, xw, xk, xv, xa, xg = [h + xx * mu[j] for j in range(6)]
    r = xr @ w_rkv[0]
    k = xk @ w_rkv[1]
    v = xv @ w_rkv[2]
    w = -jax.nn.softplus(-(w0 + jnp.tanh(xw @ w1) @ w2)) - 0.5
    decay = jnp.exp(-jnp.exp(w.astype(jnp.float32)))
    if vres is None:
        v_first = v
    else:
        v0, v1, v2 = vres
        v = v + (v_first - v) * jax.nn.sigmoid(v0 + (xv @ v1) @ v2)
    a = jax.nn.sigmoid(a0 + (xa @ a1) @ a2)
    g = jax.nn.sigmoid(xg @ g1) @ g2
    kk = heads(k * k_k).astype(jnp.float32)
    kk = kk / jnp.maximum(jnp.linalg.norm(kk, axis=-1, keepdims=True), 1e-12)
    k = k * (1.0 + (a - 1.0) * k_a)
    rh, kh, vh = heads(r), heads(k), heads(v)
    y = rwkv7_recurrence(rh, heads(decay), kh, vh, -kk, kk * heads(a))
    mean = jnp.mean(y, axis=-1, keepdims=True)
    var = jnp.mean(jnp.square(y - mean), axis=-1, keepdims=True)
    y = ((y - mean) * lax.rsqrt(var + GN_EPS)).reshape(Bsz, T, C) * ln_w + ln_b
    bonus = jnp.sum(rh * kh * r_k, axis=-1, keepdims=True) * vh
    y = y.astype(h.dtype) + bonus.reshape(Bsz, T, C)
    return (y * g) @ w_o, v_first


def dilated_group_attention(q, k, v, window, dilation):
    Bsz, T, H, Dh = q.shape
    blk = window // dilation
    Tp = -(-T // window) * window
    M = Tp // dilation
    NB = M // blk

    def to_blocks(t):
        t = jnp.pad(t, ((0, 0), (0, Tp - T), (0, 0), (0, 0)))
        t = t.reshape(Bsz, M, dilation, H, Dh).transpose(0, 2, 1, 3, 4)
        return t.reshape(Bsz, dilation, NB, blk, H, Dh)

    def with_prev(t):
        prev = jnp.pad(t, ((0, 0), (0, 0), (1, 0), (0, 0), (0, 0), (0, 0)))[:, :, :-1]
        return jnp.concatenate([prev, t], axis=3)

    def from_blocks(t):
        t = t.reshape((Bsz, dilation, M) + t.shape[4:])
        t = jnp.swapaxes(t, 1, 2).reshape((Bsz, Tp) + t.shape[3:])
        return t[:, :T]

    qb, kb, vb = to_blocks(q), to_blocks(k), to_blocks(v)
    ks, vs = with_prev(kb), with_prev(vb)
    s = jnp.einsum('brnqhd,brnkhd->brnhqk', qb, ks).astype(jnp.float32)
    qi = jnp.arange(blk)[:, None]
    kj = jnp.arange(2 * blk)[None, :]
    band = (kj >= qi) & (kj <= qi + blk)
    mask = band[None] & ((jnp.arange(NB)[:, None, None] > 0) | (kj >= blk)[None])
    s = jnp.where(mask[None, None, :, None], s, NEG_INF)
    m = jnp.max(s, axis=-1, keepdims=True)
    p = jnp.exp(s - m)
    l = jnp.sum(p, axis=-1, keepdims=True)
    o = jnp.einsum('brnhqk,brnkhd->brnqhd', p / l, vs.astype(jnp.float32))
    lse = jnp.swapaxes((m + jnp.log(l))[..., 0], -1, -2)
    return from_blocks(o), from_blocks(lse)


def shared_kv(x, kv_norm, w_kv, k_gain):
    Bsz, T, _ = x.shape
    kv = (rms_norm(x, kv_norm) @ w_kv).reshape(Bsz, T, 2, N_GROUPS, ATTN_HEADS, ATTN_HEAD_DIM)
    k = rms_norm(kv[:, :, 0], k_gain[:, None, :])
    return k, kv[:, :, 1]


def dilated_attention_mixer(h, k_sh, v_sh, w_q, q_gain, w_o):
    Bsz, T, _ = h.shape
    q = (h @ w_q).reshape(Bsz, T, N_GROUPS, ATTN_HEADS, ATTN_HEAD_DIM)
    q = rms_norm(q, q_gain[:, None, :]) * ATTN_HEAD_DIM ** -0.5
    outs, lses = [], []
    for g, (win, dil) in enumerate(DIL_GROUPS):
        o, lse = dilated_group_attention(q[:, :, g], k_sh[:, :, g], v_sh[:, :, g], win, dil)
        outs.append(o)
        lses.append(lse)
    alpha = jax.nn.softmax(jnp.stack(lses, axis=0), axis=0)
    o = jnp.sum(alpha[..., None] * jnp.stack(outs, axis=0), axis=0)
    return o.reshape(Bsz, T, ATTN_HEADS * ATTN_HEAD_DIM).astype(h.dtype) @ w_o


def setup_inputs(seed: int = 0) -> dict:
    key = jax.random.key(seed)
    ks = iter(jax.random.split(key, 40))
    f32 = jnp.float32
    D, F, NA, NB = D_MODEL, D_FF, N_A_LAYERS, N_B_LAYERS
    H, N = RWKV_HEADS, RWKV_HEAD_DIM

    def nrm(shape, scale):
        return jax.random.normal(next(ks), shape, f32) * scale

    def gain(shape):
        return 1.0 + nrm(shape, 0.02)

    def unif(shape, lo, hi):
        return jax.random.uniform(next(ks), shape, f32, lo, hi)

    return {
        "x": nrm((BATCH, SEQ, D), 1.0),
        "ffn_norm": gain((DEPTH, 2, D)),
        "ffn_w_in": nrm((DEPTH, 2, D, 2 * F), D ** -0.5),
        "ffn_w_out": nrm((DEPTH, 2, F, D), F ** -0.5),
        "mix_norm": gain((DEPTH, D)),
        "rwkv_mu": unif((NA, 6, D), 0.0, 1.0),
        "rwkv_w_rkv": nrm((NA, 3, D, D), D ** -0.5),
        "rwkv_w0": unif((NA, D), -5.0, -1.0),
        "rwkv_w1": nrm((NA, D, D_DECAY_LORA), D ** -0.5),
        "rwkv_w2": nrm((NA, D_DECAY_LORA, D), 0.1 * D_DECAY_LORA ** -0.5),
        "rwkv_a0": nrm((NA, D), 0.1),
        "rwkv_a1": nrm((NA, D, D_AAA_LORA), D ** -0.5),
        "rwkv_a2": nrm((NA, D_AAA_LORA, D), 0.5 * D_AAA_LORA ** -0.5),
        "rwkv_v0": 1.0 + nrm((NA - 1, D), 0.1),
        "rwkv_v1": nrm((NA - 1, D, D_MV_LORA), D ** -0.5),
        "rwkv_v2": nrm((NA - 1, D_MV_LORA, D), 0.5 * D_MV_LORA ** -0.5),
        "rwkv_g1": nrm((NA, D, D_GATE_LORA), D ** -0.5),
        "rwkv_g2": nrm((NA, D_GATE_LORA, D), D_GATE_LORA ** -0.5),
        "rwkv_k_k": 0.85 + nrm((NA, D), 0.05),
        "rwkv_k_a": 1.0 + nrm((NA, D), 0.05),
        "rwkv_r_k": nrm((NA, H, N), 0.1),
        "rwkv_ln_w": gain((NA, D)),
        "rwkv_ln_b": nrm((NA, D), 0.02),
        "rwkv_w_o": nrm((NA, D, D), D ** -0.5),
        "kv_norm": gain((D,)),
        "w_kv": nrm((D, 2 * Q_WIDTH), D ** -0.5),
        "k_norm": gain((N_GROUPS, ATTN_HEAD_DIM)),
        "attn_w_q": nrm((NB, D, Q_WIDTH), D ** -0.5),
        "q_norm": gain((NB, N_GROUPS, ATTN_HEAD_DIM)),
        "attn_w_o": nrm((NB, ATTN_HEADS * ATTN_HEAD_DIM, D), (ATTN_HEADS * ATTN_HEAD_DIM) ** -0.5),
    }


def reference(x, ffn_norm, ffn_w_in, ffn_w_out, mix_norm, rwkv_mu, rwkv_w_rkv, rwkv_w0,
              rwkv_w1, rwkv_w2, rwkv_a0, rwkv_a1, rwkv_a2, rwkv_v0, rwkv_v1, rwkv_v2,
              rwkv_g1, rwkv_g2, rwkv_k_k, rwkv_k_a, rwkv_r_k, rwkv_ln_w, rwkv_ln_b, rwkv_w_o,
              kv_norm, w_kv, k_norm, attn_w_q, q_norm, attn_w_o):
    v_first = None
    k_sh = None
    v_sh = None
    for l in range(DEPTH):
        x = x + 0.5 * swiglu(rms_norm(x, ffn_norm[l, 0]), ffn_w_in[l, 0], ffn_w_out[l, 0])
        h = rms_norm(x, mix_norm[l])
        if l < N_A_LAYERS:
            vres = None if l == 0 else (rwkv_v0[l - 1], rwkv_v1[l - 1], rwkv_v2[l - 1])
            y, v_first = rwkv7_time_mix(h, v_first, vres, rwkv_mu[l], rwkv_w_rkv[l], rwkv_w0[l],
                                        rwkv_w1[l], rwkv_w2[l], rwkv_a0[l], rwkv_a1[l], rwkv_a2[l],
                                        rwkv_g1[l], rwkv_g2[l], rwkv_k_k[l], rwkv_k_a[l], rwkv_r_k[l],
                                        rwkv_ln_w[l], rwkv_ln_b[l], rwkv_w_o[l])
        else:
            i = l - N_A_LAYERS
            y = dilated_attention_mixer(h, k_sh, v_sh, attn_w_q[i], q_norm[i], attn_w_o[i])
        x = x + y
        x = x + 0.5 * swiglu(rms_norm(x, ffn_norm[l, 1]), ffn_w_in[l, 1], ffn_w_out[l, 1])
        if l == N_A_LAYERS - 1:
            k_sh, v_sh = shared_kv(x, kv_norm, w_kv, k_norm)
    return x
```

```python
import functools

import jax
import jax.numpy as jnp
from jax import lax
from jax.experimental import pallas as pl
from jax.experimental.pallas import tpu as pltpu

F32 = jnp.float32
BF16 = jnp.bfloat16

RMS_EPS = 1e-6
GN_EPS = 64e-5
HEAD_DIM = 64
LANES = 128
DIL_GROUPS = ((128, 1), (512, 4), (2048, 16))
ATT_BLK = 128
NEG_INF = -1e30
CHUNK = 64
TOKEN_TILE = 512
FFN_COL_TILE = 256
VMEM_LIMIT = 56 * 1024 * 1024


def _params(n_axes, parallel=True):
    sem = ("parallel" if parallel else "arbitrary",) * n_axes
    return pltpu.CompilerParams(dimension_semantics=sem, vmem_limit_bytes=VMEM_LIMIT)


def _mm(a, b):
    return jnp.dot(a.astype(BF16), b.astype(BF16), preferred_element_type=F32)


def _split2(x):
    hi = x.astype(BF16)
    lo = (x - hi.astype(F32)).astype(BF16)
    return hi, lo


def _mm3(a, b):
    ah, al = _split2(a)
    bh, bl = _split2(b)
    d = lambda p, q: jnp.dot(p, q, preferred_element_type=F32)
    return d(ah, bh) + (d(ah, bl) + d(al, bh))


def _mm3_nt(a, b):
    ah, al = _split2(a)
    bh, bl = _split2(b)
    dn = (((1,), (1,)), ((), ()))
    d = lambda p, q: lax.dot_general(p, q, dn, preferred_element_type=F32)
    return d(ah, bh) + (d(ah, bl) + d(al, bh))


def _mm_exact_rhs(a, b_bf16):
    ah, al = _split2(a)
    d = lambda p: jnp.dot(p, b_bf16, preferred_element_type=F32)
    return d(ah) + d(al)


def _rms(x, g):
    return x * lax.rsqrt(jnp.mean(x * x, axis=-1, keepdims=True) + RMS_EPS) * g


def _sigmoid(x):
    return 1.0 / (1.0 + jnp.exp(-x))


def _full(shape):
    n = len(shape)
    return pl.BlockSpec(shape, lambda *_: (0,) * n)


def _resident(shape):
    n = len(shape)
    return pl.BlockSpec(shape, lambda *_: (0,) * n, pipeline_mode=pl.Buffered(1))


def _ffn_kernel(x_ref, g_ref, win_ref, wout_ref, o_ref, act_ref):
    x = x_ref[...]
    hn = _rms(x, g_ref[...]).astype(BF16)
    d_ff = wout_ref.shape[0]
    for c in range(d_ff // FFN_COL_TILE):
        lo = c * FFN_COL_TILE
        gate = jnp.dot(hn, win_ref[:, lo:lo + FFN_COL_TILE], preferred_element_type=F32)
        up = jnp.dot(hn, win_ref[:, d_ff + lo:d_ff + lo + FFN_COL_TILE],
                     preferred_element_type=F32)
        act_ref[:, lo:lo + FFN_COL_TILE] = (gate * _sigmoid(gate) * up).astype(BF16)
    o_ref[...] = x + 0.5 * jnp.dot(act_ref[...], wout_ref[...], preferred_element_type=F32)


def _ffn(x, norms, w_in, w_out, idx):
    t, d = x.shape
    d_ff = w_out.shape[1]
    tm = TOKEN_TILE
    return pl.pallas_call(
        _ffn_kernel,
        grid=(t // tm,),
        in_specs=[
            pl.BlockSpec((tm, d), lambda i: (i, 0)),
            pl.BlockSpec((None, 1, d), lambda i: (idx, 0, 0)),
            pl.BlockSpec((None, d, 2 * d_ff), lambda i: (idx, 0, 0), pipeline_mode=pl.Buffered(1)),
            pl.BlockSpec((None, d_ff, d), lambda i: (idx, 0, 0), pipeline_mode=pl.Buffered(1)),
        ],
        out_specs=pl.BlockSpec((tm, d), lambda i: (i, 0)),
        out_shape=jax.ShapeDtypeStruct((t, d), F32),
        scratch_shapes=[pltpu.VMEM((tm, d_ff), BF16)],
        compiler_params=_params(1),
        name="ffn",
    )(x, norms, w_in, w_out)


def _rwkv_proj_kernel(*refs, has_vres):
    if has_vres:
        (x_ref, xp_ref, gn_ref, mu_ref, wrkv_ref, w0_ref, w1_ref, w2_ref, a0_ref, a1_ref,
         a2_ref, g1_ref, g2_ref, kk_ref, ka_ref, e_ref, v0_ref, v1_ref, v2_ref, vf_ref,
         r_o, ld_o, k_o, v_o, na_o, nb_o, g_o) = refs
    else:
        (x_ref, xp_ref, gn_ref, mu_ref, wrkv_ref, w0_ref, w1_ref, w2_ref, a0_ref, a1_ref,
         a2_ref, g1_ref, g2_ref, kk_ref, ka_ref, e_ref,
         r_o, ld_o, k_o, v_o, na_o, nb_o, g_o) = refs
    i = pl.program_id(0)
    gn = gn_ref[...]
    h = _rms(x_ref[...], gn)
    hp = _rms(xp_ref[7:8, :], gn) * jnp.where(i > 0, 1.0, 0.0)
    rows = lax.broadcasted_iota(jnp.int32, h.shape, 0)
    hs = jnp.where(rows == 0, hp, pltpu.roll(h, 1, axis=0))
    xx = hs - h
    mu = mu_ref[...]
    xr, xw, xk, xv, xa, xg = [h + xx * mu[j:j + 1] for j in range(6)]
    r = _mm(xr, wrkv_ref[0])
    k = _mm(xk, wrkv_ref[1])
    v = _mm(xv, wrkv_ref[2])
    w = w0_ref[...] + _mm(jnp.tanh(_mm(xw, w1_ref[...])), w2_ref[...])
    z = -w
    wl = -(jnp.maximum(z, 0.0) + jnp.log(1.0 + jnp.exp(-jnp.abs(z)))) - 0.5
    ld_o[...] = -jnp.exp(wl)
    a = _sigmoid(a0_ref[...] + _mm(_mm(xa, a1_ref[...]), a2_ref[...]))
    g_o[...] = _mm(_sigmoid(_mm(xg, g1_ref[...])), g2_ref[...])
    if has_vres:
        v = v + (vf_ref[...] - v) * _sigmoid(v0_ref[...] + _mm(_mm(xv, v1_ref[...]), v2_ref[...]))
    kkr = k * kk_ref[...]
    norm = jnp.sqrt(_mm_exact_rhs(kkr * kkr, e_ref[...]))
    kk = kkr / jnp.maximum(norm, 1e-12)
    r_o[...] = r
    k_o[...] = k * (1.0 + (a - 1.0) * ka_ref[...])
    v_o[...] = v
    na_o[...] = -kk
    nb_o[...] = kk * a


def _rwkv_proj(x, gn, mu, wrkv, w0, w1, w2, a0, a1, a2, g1, g2, k_k, k_a, e_seg, vres):
    t, d = x.shape
    tm = TOKEN_TILE // 2
    row = pl.BlockSpec((tm, d), lambda i: (i, 0))
    prev =pl.BlockSpec((8, d), lambda i: (jnp.maximum(i * (tm // 8) - 1, 0), 0))
    args = [x, x, gn, mu, wrkv, w0, w1, w2, a0, a1, a2, g1, g2, k_k, k_a, e_seg]
    specs = [row, prev, _full(gn.shape), _full(mu.shape), _resident(wrkv.shape), _full(w0.shape),
             _full(w1.shape), _full(w2.shape), _full(a0.shape), _full(a1.shape), _full(a2.shape),
             _full(g1.shape), _full(g2.shape), _full(k_k.shape), _full(k_a.shape),
             _resident(e_seg.shape)]
    if vres is not None:
        v0, v1, v2, v_first = vres
        args += [v0, v1, v2, v_first]
        specs += [_full(v0.shape), _full(v1.shape), _full(v2.shape), row]
    return pl.pallas_call(
        functools.partial(_rwkv_proj_kernel, has_vres=vres is not None),
        grid=(t // tm,),
        in_specs=specs,
        out_specs=[row] * 7,
        out_shape=[jax.ShapeDtypeStruct((t, d), F32)] * 7,
        compiler_params=_params(1),
        name="rwkv_proj",
    )(*args)


def _rwkv_scan_kernel(r_ref, ld_ref, k_ref, v_ref, na_ref, nb_ref, y_ref, s_ref):
    c = pl.program_id(0)
    n_pairs = r_ref.shape[1] // LANES
    L = CHUNK
    assert 2 * L == LANES and L == HEAD_DIM

    @pl.when(c == 0)
    def _():
        s_ref[...] = jnp.zeros_like(s_ref)

    ri = lax.broadcasted_iota(jnp.int32, (2 * L, 2 * L), 0)
    ci = lax.broadcasted_iota(jnp.int32, (2 * L, 2 * L), 1)
    same_head = (ri < L) == (ci < L)
    strict = (ci & (L - 1)) < (ri & (L - 1))
    incl = (ci & (L - 1)) <= (ri & (L - 1))
    lane = lax.broadcasted_iota(jnp.int32, (L, LANES), 1)
    head0 = lane < HEAD_DIM
    lane2 = lax.broadcasted_iota(jnp.int32, (2 * L, LANES), 1)
    tri = (lax.broadcasted_iota(jnp.int32, (L, L), 0)
           >= lax.broadcasted_iota(jnp.int32, (L, L), 1)).astype(BF16)

    def pair_body(p, carry):
        sl = pl.ds(pl.multiple_of(p * LANES, LANES), LANES)
        r = r_ref[:, sl]
        ld = ld_ref[:, sl]
        k = k_ref[:, sl]
        v = v_ref[:, sl]
        na = na_ref[:, sl]
        nb = nb_ref[:, sl]
        s_old = s_ref[p]

        l1 = ld.astype(BF16)
        rem = ld - l1.astype(F32)
        l2 = rem.astype(BF16)
        l3 = (rem - l2.astype(F32)).astype(BF16)
        cum3 = jnp.dot(tri, jnp.concatenate([l1, l2, l3], axis=1), preferred_element_type=F32)
        cum = cum3[:, :LANES] + cum3[:, LANES:2 * LANES] + cum3[:, 2 * LANES:]
        cum_last = cum[L - 1:L, :]
        w_inv = jnp.exp(-cum)
        w_rel = jnp.exp(cum_last - cum)
        rt = r * jnp.exp(cum)
        at = na * jnp.exp(cum - ld)
        bt = nb * w_inv
        kt = k * w_inv

        x = jnp.concatenate([rt, at], axis=0)
        x0 = jnp.where(lane2 < HEAD_DIM, x, 0.0)
        x1 = jnp.where(lane2 < HEAD_DIM, 0.0, x)
        g0 = _mm3_nt(x0, jnp.concatenate([bt, kt], axis=0))
        g1 = _mm3_nt(x1, jnp.concatenate([kt, bt], axis=0))
        q = jnp.concatenate([g0[L:], g1[L:]], axis=0)
        m = jnp.concatenate([g0[:L], g1[:L]], axis=0)
        nil = jnp.where(same_head & strict, q, 0.0)
        a_k = jnp.where(jnp.logical_not(same_head) & strict, q, 0.0)
        m_u = jnp.where(same_head & incl, m, 0.0)
        m_v = jnp.where(jnp.logical_not(same_head) & incl, m, 0.0)

        xs = _mm3_nt(x, s_old)
        vv = jnp.concatenate([v, v], axis=0)
        u = jnp.concatenate([xs[L:], xs[L:]], axis=0) + _mm3(a_k, vv)
        pw = nil
        for _ in range(5):
            both = _mm3(pw, jnp.concatenate([u, pw], axis=1))
            u = u + both[:, :LANES]
            pw = both[:, LANES:]
        u = u + _mm3(pw, u)
        u = jnp.where(head0, u[:L], u[L:])

        uu = jnp.concatenate([u, u], axis=0)
        ym = _mm3(jnp.concatenate([m_u, m_v], axis=1), jnp.concatenate([uu, vv], axis=0))
        y = jnp.concatenate([xs[:L], xs[:L]], axis=0) + ym
        y_ref[:, sl] = jnp.where(head0, y[:L], y[L:])

        keys = jnp.concatenate([nb * w_rel, k * w_rel], axis=0)
        vals = jnp.concatenate([u, v], axis=0)
        upd = _mm3(vals.T, keys)
        s_ref[p] = jnp.where(same_head, s_old * jnp.exp(cum_last) + upd, 0.0)
        return carry

    lax.fori_loop(0, n_pairs, pair_body, 0)


def _rwkv_scan(r, ld, k, v, na, nb):
    t, d = r.shape
    blk = pl.BlockSpec((CHUNK, d), lambda c: (c, 0))
    return pl.pallas_call(
        _rwkv_scan_kernel,
        grid=(t // CHUNK,),
        in_specs=[blk] * 6,
        out_specs=blk,
        out_shape=jax.ShapeDtypeStruct((t, d), F32),
        scratch_shapes=[pltpu.VMEM((d // LANES, LANES, LANES), F32)],
        compiler_params=_params(1, parallel=False),
        name="rwkv_scan",
    )(r, ld, k, v, na, nb)


def _rwkv_post_kernel(y_ref, r_ref, k_ref, v_ref, g_ref, x_ref, lnw_ref, lnb_ref, rk_ref,
                      wo_ref, e_ref, o_ref):
    e = e_ref[...]
    y = y_ref[...]
    inv_n = 1.0 / HEAD_DIM
    yc = y - _mm_exact_rhs(y, e) * inv_n
    var = _mm_exact_rhs(yc * yc, e) * inv_n
    yn = yc * lax.rsqrt(var + GN_EPS) * lnw_ref[...] + lnb_ref[...]
    v = v_ref[...]
    bonus = _mm_exact_rhs(r_ref[...] * k_ref[...] * rk_ref[...], e) * v
    z = (yn + bonus) * g_ref[...]
    o_ref[...] = x_ref[...] + _mm(z, wo_ref[...])


def _rwkv_post(y, r, k, v, g, x, ln_w, ln_b, r_k, w_o, e_seg):
    t, d = x.shape
    tm = TOKEN_TILE
    row = pl.BlockSpec((tm, d), lambda i: (i, 0))
    return pl.pallas_call(
        _rwkv_post_kernel,
        grid=(t // tm,),
        in_specs=[row] * 6 + [_full(ln_w.shape), _full(ln_b.shape), _full(r_k.shape),
                              _resident(w_o.shape), _resident(e_seg.shape)],
        out_specs=row,
        out_shape=jax.ShapeDtypeStruct((t, d), F32),
        compiler_params=_params(1),
        name="rwkv_post",
    )(y, r, k, v, g, x, ln_w, ln_b, r_k, w_o, e_seg)


def _head_rms(y, e, gain):
    ms = _mm_exact_rhs(y * y, e) * (1.0 / HEAD_DIM)
    return y * lax.rsqrt(ms + RMS_EPS) * gain


def _kv_proj_kernel(x_ref, gn_ref, w_ref, gain_ref, e_ref, o_ref, hn_ref):
    j = pl.program_id(1)
    n_groups = len(DIL_GROUPS)

    @pl.when(j == 0)
    def _():
        hn_ref[...] = _rms(x_ref[...], gn_ref[...]).astype(BF16)

    y = jnp.dot(hn_ref[...], w_ref[...], preferred_element_type=F32)

    @pl.when(j < n_groups)
    def _():
        o_ref[...] = _head_rms(y, e_ref[...], gain_ref[...]).astype(BF16)

    @pl.when(j >= n_groups)
    def _():
        o_ref[...] = y.astype(BF16)


def _kv_proj(x, gn, w_kv, gains, e_seg):
    t, d = x.shape
    tm = TOKEN_TILE
    n_blk = w_kv.shape[1] // d
    return pl.pallas_call(
        _kv_proj_kernel,
        grid=(t // tm, n_blk),
        in_specs=[
            pl.BlockSpec((tm, d), lambda i, j: (i, 0)),
            _full(gn.shape),
            pl.BlockSpec((d, d), lambda i, j: (0, j)),
            pl.BlockSpec((None, 1, d), lambda i, j: (jnp.minimum(j, len(DIL_GROUPS) - 1), 0, 0)),
            _resident(e_seg.shape),
        ],
        out_specs=pl.BlockSpec((None, tm, d), lambda i, j: (j, i, 0)),
        out_shape=jax.ShapeDtypeStruct((n_blk, t, d), BF16),
        scratch_shapes=[pltpu.VMEM((tm, d), BF16)],
        compiler_params=pltpu.CompilerParams(dimension_semantics=("parallel", "arbitrary"),
                                             vmem_limit_bytes=VMEM_LIMIT),
        name="kv_proj",
    )(x, gn, w_kv, gains, e_seg)


def _q_proj_kernel(x_ref, gn_ref, w_ref, gain_ref, e_ref, o_ref, hn_ref):
    @pl.when(pl.program_id(1) == 0)
    def _():
        hn_ref[...] = _rms(x_ref[...], gn_ref[...]).astype(BF16)

    y = jnp.dot(hn_ref[...], w_ref[...], preferred_element_type=F32)
    o_ref[...] = (_head_rms(y, e_ref[...], gain_ref[...]) * HEAD_DIM ** -0.5).astype(BF16)


def _q_proj(x, gn, w_q, gains, e_seg):
    t, d = x.shape
    tm = TOKEN_TILE
    n_blk = w_q.shape[1] // d
    return pl.pallas_call(
        _q_proj_kernel,
        grid=(t // tm, n_blk),
        in_specs=[
            pl.BlockSpec((tm, d), lambda i, j: (i, 0)),
            _full(gn.shape),
            pl.BlockSpec((d, d), lambda i, j: (0, j)),
            pl.BlockSpec((None, 1, d), lambda i, j: (j, 0, 0)),
            _resident(e_seg.shape),
        ],
        out_specs=pl.BlockSpec((None, tm, d), lambda i, j: (j, i, 0)),
        out_shape=jax.ShapeDtypeStruct((n_blk, t, d), BF16),
        scratch_shapes=[pltpu.VMEM((tm, d), BF16)],
        compiler_params=pltpu.CompilerParams(dimension_semantics=("parallel", "arbitrary"),
                                             vmem_limit_bytes=VMEM_LIMIT),
        name="q_proj",
    )(x, gn, w_q, gains, e_seg)


def _dil_attn_kernel(q_ref, kp_ref, kc_ref, vp_ref, vc_ref, o_ref, m_ref, l_ref):
    n = pl.program_id(1)
    blk = ATT_BLK
    n_pairs = q_ref.shape[1] // LANES
    qi = lax.broadcasted_iota(jnp.int32, (blk, 2 * blk), 0)
    kj = lax.broadcasted_iota(jnp.int32, (blk, 2 * blk), 1)
    valid = (kj >= qi) & (kj <= qi + blk) & ((n > 0) | (kj >= blk))
    lane = lax.broadcasted_iota(jnp.int32, (blk, LANES), 1)
    head0 = lane < HEAD_DIM
    dn_nt = (((1,), (1,)), ((), ()))

    def pair_body(p, carry):
        m_acc, l_acc = carry
        sl = pl.ds(pl.multiple_of(p * LANES, LANES), LANES)
        qp = q_ref[:, sl]
        kcat = jnp.concatenate([kp_ref[:, sl], kc_ref[:, sl]], axis=0)
        vcat = jnp.concatenate([vp_ref[:, sl], vc_ref[:, sl]], axis=0)
        outs = []
        for hh in range(2):
            qm = jnp.where(head0 if hh == 0 else jnp.logical_not(head0), qp, jnp.zeros_like(qp))
            s = lax.dot_general(qm, kcat, dn_nt, preferred_element_type=F32)
            s = jnp.where(valid, s, NEG_INF)
            mx = jnp.max(s, axis=-1, keepdims=True)
            pe = jnp.exp(s - mx)
            ls = jnp.sum(pe, axis=-1, keepdims=True)
            outs.append(jnp.dot(pe.astype(BF16), vcat, preferred_element_type=F32))
            hit = lane == 2 * p + hh
            m_acc = jnp.where(hit, mx, m_acc)
            l_acc = jnp.where(hit, ls, l_acc)
        o_ref[:, sl] = jnp.where(head0, outs[0], outs[1])
        return m_acc, l_acc

    init = (jnp.zeros((blk, LANES), F32), jnp.ones((blk, LANES), F32))
    m_acc, l_acc = lax.fori_loop(0, n_pairs, pair_body, init)
    m_ref[...] = m_acc
    l_ref[...] = l_acc


def _dil_attn(q, kv, group):
    n_groups, t, d = q.shape
    _, dil = DIL_GROUPS[group]
    m_rows = t // dil
    qv = q.reshape(n_groups, m_rows, dil * d)
    kvv = kv.reshape(2 * n_groups, m_rows, dil * d)
    blk = ATT_BLK
    cur = lambda a: pl.BlockSpec((None, blk, d), lambda r, n: (a, n, r))
    prev = lambda a: pl.BlockSpec((None, blk, d), lambda r, n: (a, jnp.maximum(n - 1, 0), r))
    o, m, l = pl.pallas_call(
        _dil_attn_kernel,
        grid=(dil, m_rows // blk),
        in_specs=[cur(group), prev(group), cur(group), prev(n_groups + group), cur(n_groups + group)],
        out_specs=[pl.BlockSpec((blk, d), lambda r, n: (n, r)),
                   pl.BlockSpec((blk, LANES), lambda r, n: (n, r)),
                   pl.BlockSpec((blk, LANES), lambda r, n: (n, r))],
        out_shape=[jax.ShapeDtypeStruct((m_rows, dil * d), F32),
                   jax.ShapeDtypeStruct((m_rows, dil * LANES), F32),
                   jax.ShapeDtypeStruct((m_rows, dil * LANES), F32)],
        compiler_params=_params(2),
        name="dil_attn",
    )(qv, kvv, kvv, kvv, kvv)
    return o.reshape(t, d), m.reshape(t, LANES), l.reshape(t, LANES)


def _attn_combine_kernel(o0, o1, o2, m0, m1, m2, l0, l1, l2, x_ref, wo_ref, ex_ref, out_ref):
    ms = [m0[...], m1[...], m2[...]]
    top = jnp.maximum(jnp.maximum(ms[0], ms[1]), ms[2])
    es = [jnp.exp(mg - top) for mg in ms]
    den = es[0] * l0[...] + es[1] * l1[...] + es[2] * l2[...]
    ex = ex_ref[...]
    acc = None
    for eg, og in zip(es, (o0, o1, o2)):
        coef = _mm_exact_rhs(eg / den, ex)
        term = coef * og[...]
        acc = term if acc is None else acc + term
    out_ref[...] = x_ref[...] + _mm(acc, wo_ref[...])


def _attn_combine(parts, x, w_o, expand):
    t, d = x.shape
    tm = TOKEN_TILE
    row = pl.BlockSpec((tm, d), lambda i: (i, 0))
    stat = pl.BlockSpec((tm, LANES), lambda i: (i, 0))
    os_, ms_, ls_ = zip(*parts)
    return pl.pallas_call(
        _attn_combine_kernel,
        grid=(t // tm,),
        in_specs=[row] * 3 + [stat] * 6 + [row, _resident(w_o.shape), _full(expand.shape)],
        out_specs=row,
        out_shape=jax.ShapeDtypeStruct((t, d), F32),
        compiler_params=_params(1),
        name="attn_combine",
    )(*os_, *ms_, *ls_, x, w_o, expand)


def kernel(x, ffn_norm, ffn_w_in, ffn_w_out, mix_norm, rwkv_mu, rwkv_w_rkv, rwkv_w0, rwkv_w1, rwkv_w2, rwkv_a0, rwkv_a1, rwkv_a2, rwkv_v0, rwkv_v1, rwkv_v2, rwkv_g1, rwkv_g2, rwkv_k_k, rwkv_k_a, rwkv_r_k, rwkv_ln_w, rwkv_ln_b, rwkv_w_o, kv_norm, w_kv, k_norm, attn_w_q, q_norm, attn_w_o):
    bsz, t, d = x.shape
    assert bsz == 1 and d % LANES == 0 and t % max(w for w, _ in DIL_GROUPS) == 0
    depth = ffn_norm.shape[0]
    n_a = rwkv_mu.shape[0]
    n_groups = len(DIL_GROUPS)
    row = lambda p: p.reshape(1, -1)

    ffn_g = ffn_norm.reshape(2 * depth, 1, d)
    ffn_wi = ffn_w_in.astype(BF16).reshape(2 * depth, d, -1)
    ffn_wo = ffn_w_out.astype(BF16).reshape(2 * depth, -1, d)
    head_of_lane = jnp.arange(d) // HEAD_DIM
    e_seg = (head_of_lane[:, None] == head_of_lane[None, :]).astype(BF16)
    expand = (jnp.arange(LANES)[:, None] == head_of_lane[None, :]).astype(BF16)

    xs = x.reshape(t, d)
    v_first = None
    kv = None
    for l in range(depth):
        xs = _ffn(xs, ffn_g, ffn_wi, ffn_wo, 2 * l)
        if l < n_a:
            vres = None
            if l > 0:
                vres = (row(rwkv_v0[l - 1]), rwkv_v1[l - 1].astype(BF16),
                        rwkv_v2[l - 1].astype(BF16), v_first)
            r, ld, k, v, na, nb, g = _rwkv_proj(
                xs, row(mix_norm[l]), rwkv_mu[l], rwkv_w_rkv[l].astype(BF16), row(rwkv_w0[l]),
                rwkv_w1[l].astype(BF16), rwkv_w2[l].astype(BF16), row(rwkv_a0[l]),
                rwkv_a1[l].astype(BF16), rwkv_a2[l].astype(BF16), rwkv_g1[l].astype(BF16),
                rwkv_g2[l].astype(BF16), row(rwkv_k_k[l]), row(rwkv_k_a[l]), e_seg, vres)
            if l == 0:
                v_first = v
            y = _rwkv_scan(r, ld, k, v, na, nb)
            xs = _rwkv_post(y, r, k, v, g, xs, row(rwkv_ln_w[l]), row(rwkv_ln_b[l]),
                            row(rwkv_r_k[l]), rwkv_w_o[l].astype(BF16), e_seg)
        else:
            i = l - n_a
            q_gain = jnp.tile(q_norm[i], (1, d // HEAD_DIM)).reshape(n_groups, 1, d)
            q = _q_proj(xs, row(mix_norm[l]), attn_w_q[i].astype(BF16), q_gain, e_seg)
            parts = [_dil_attn(q, kv, grp) for grp in range(n_groups)]
            xs = _attn_combine(parts, xs, attn_w_o[i].astype(BF16), expand)
        xs = _ffn(xs, ffn_g, ffn_wi, ffn_wo, 2 * l + 1)
        if l == n_a - 1:
            k_gain = jnp.tile(k_norm, (1, d // HEAD_DIM)).reshape(n_groups, 1, d)
            kv = _kv_proj(xs, row(kv_norm), w_kv.astype(BF16), k_gain, e_seg)
    return xs.reshape(bsz, t, d)
```

```python
import functools

import jax
import jax.numpy as jnp
from jax import lax
from jax.experimental import pallas as pl
from jax.experimental.pallas import tpu as pltpu

F32 = jnp.float32
BF16 = jnp.bfloat16

RMS_EPS = 1e-6
GN_EPS = 64e-5
HEAD_DIM = 64
LANES = 128
DIL_GROUPS = ((128, 1), (512, 4), (2048, 16))
ATT_BLK = 128
NEG_INF = -1e30
CHUNK = 64
TOKEN_TILE = 512
FFN_COL_TILE = 256
VMEM_LIMIT = 56 * 1024 * 1024


def _params(n_axes, parallel=True):
    sem = ("parallel" if parallel else "arbitrary",) * n_axes
    return pltpu.CompilerParams(dimension_semantics=sem, vmem_limit_bytes=VMEM_LIMIT)


def _mm(a, b):
    return jnp.dot(a.astype(BF16), b.astype(BF16), preferred_element_type=F32)


def _split2(x):
    hi = x.astype(BF16)
    lo = (x - hi.astype(F32)).astype(BF16)
    return hi, lo


def _mm_nt(a, b):
    return lax.dot_general(a.astype(BF16), b.astype(BF16), (((1,), (1,)), ((), ())),
                           preferred_element_type=F32)


def _mm_exact_rhs(a, b_bf16):
    ah, al = _split2(a)
    d = lambda p: jnp.dot(p, b_bf16, preferred_element_type=F32)
    return d(ah) + d(al)


def _rms(x, g):
    return x * lax.rsqrt(jnp.mean(x * x, axis=-1, keepdims=True) + RMS_EPS) * g


def _sigmoid(x):
    return 1.0 / (1.0 + jnp.exp(-x))


def _full(shape):
    n = len(shape)
    return pl.BlockSpec(shape, lambda *_: (0,) * n)


def _resident(shape):
    n = len(shape)
    return pl.BlockSpec(shape, lambda *_: (0,) * n, pipeline_mode=pl.Buffered(1))


def _ffn_kernel(x_ref, g_ref, win_ref, wout_ref, o_ref, act_ref):
    x = x_ref[...]
    hn = _rms(x, g_ref[...]).astype(BF16)
    d_ff = wout_ref.shape[0]
    for c in range(d_ff // FFN_COL_TILE):
        lo = c * FFN_COL_TILE
        gate = jnp.dot(hn, win_ref[:, lo:lo + FFN_COL_TILE], preferred_element_type=F32)
        up = jnp.dot(hn, win_ref[:, d_ff + lo:d_ff + lo + FFN_COL_TILE],
                     preferred_element_type=F32)
        act_ref[:, lo:lo + FFN_COL_TILE] = (gate * _sigmoid(gate) * up).astype(BF16)
    o_ref[...] = x + 0.5 * jnp.dot(act_ref[...], wout_ref[...], preferred_element_type=F32)


def _ffn(x, norms, w_in, w_out, idx):
    t, d = x.shape
    d_ff = w_out.shape[1]
    tm = TOKEN_TILE
    return pl.pallas_call(
        _ffn_kernel,
        grid=(t // tm,),
        in_specs=[
            pl.BlockSpec((tm, d), lambda i: (i, 0)),
            pl.BlockSpec((None, 1, d), lambda i: (idx, 0, 0)),
            pl.BlockSpec((None, d, 2 * d_ff), lambda i: (idx, 0, 0), pipeline_mode=pl.Buffered(1)),
            pl.BlockSpec((None, d_ff, d), lambda i: (idx, 0, 0), pipeline_mode=pl.Buffered(1)),
        ],
        out_specs=pl.BlockSpec((tm, d), lambda i: (i, 0)),
        out_shape=jax.ShapeDtypeStruct((t, d), F32),
        scratch_shapes=[pltpu.VMEM((tm, d_ff), BF16)],
        compiler_params=_params(1),
        name="ffn",
    )(x, norms, w_in, w_out)


def _rwkv_proj_kernel(*refs, has_vres):
    if has_vres:
        (x_ref, xp_ref, gn_ref, mu_ref, wrkv_ref, w0_ref, w1_ref, w2_ref, a0_ref, a1_ref,
         a2_ref, g1_ref, g2_ref, kk_ref, ka_ref, e_ref, v0_ref, v1_ref, v2_ref, vf_ref,
         r_o, ld_o, k_o, v_o, na_o, nb_o, g_o) = refs
    else:
        (x_ref, xp_ref, gn_ref, mu_ref, wrkv_ref, w0_ref, w1_ref, w2_ref, a0_ref, a1_ref,
         a2_ref, g1_ref, g2_ref, kk_ref, ka_ref, e_ref,
         r_o, ld_o, k_o, v_o, na_o, nb_o, g_o) = refs
    i = pl.program_id(0)
    gn = gn_ref[...]
    h = _rms(x_ref[...], gn)
    hp = _rms(xp_ref[7:8, :], gn) * jnp.where(i > 0, 1.0, 0.0)
    rows = lax.broadcasted_iota(jnp.int32, h.shape, 0)
    hs = jnp.where(rows == 0, hp, pltpu.roll(h, 1, axis=0))
    xx = hs - h
    mu = mu_ref[...]
    xr, xw, xk, xv, xa, xg = [h + xx * mu[j:j + 1] for j in range(6)]
    r = _mm(xr, wrkv_ref[0])
    k = _mm(xk, wrkv_ref[1])
    v = _mm(xv, wrkv_ref[2])
    w = w0_ref[...] + _mm(jnp.tanh(_mm(xw, w1_ref[...])), w2_ref[...])
    z = -w
    wl = -(jnp.maximum(z, 0.0) + jnp.log(1.0 + jnp.exp(-jnp.abs(z)))) - 0.5
    ld_o[...] = -jnp.exp(wl)
    a = _sigmoid(a0_ref[...] + _mm(_mm(xa, a1_ref[...]), a2_ref[...]))
    g_o[...] = _mm(_sigmoid(_mm(xg, g1_ref[...])), g2_ref[...])
    if has_vres:
        v = v + (vf_ref[...] - v) * _sigmoid(v0_ref[...] + _mm(_mm(xv, v1_ref[...]), v2_ref[...]))
    kkr = k * kk_ref[...]
    norm = jnp.sqrt(_mm_exact_rhs(kkr * kkr, e_ref[...]))
    kk = kkr / jnp.maximum(norm, 1e-12)
    r_o[...] = r
    k_o[...] = k * (1.0 + (a - 1.0) * ka_ref[...])
    v_o[...] = v
    na_o[...] = -kk
    nb_o[...] = kk * a


def _rwkv_proj(x, gn, mu, wrkv, w0, w1, w2, a0, a1, a2, g1, g2, k_k, k_a, e_seg, vres):
    t, d = x.shape
    tm = TOKEN_TILE // 2
    row = pl.BlockSpec((tm, d), lambda i: (i, 0))
    prev =pl.BlockSpec((8, d), lambda i: (jnp.maximum(i * (tm // 8) - 1, 0), 0))
    args = [x, x, gn, mu, wrkv, w0, w1, w2, a0, a1, a2, g1, g2, k_k, k_a, e_seg]
    specs = [row, prev, _full(gn.shape), _full(mu.shape), _resident(wrkv.shape), _full(w0.shape),
             _full(w1.shape), _full(w2.shape), _full(a0.shape), _full(a1.shape), _full(a2.shape),
             _full(g1.shape), _full(g2.shape), _full(k_k.shape), _full(k_a.shape),
             _resident(e_seg.shape)]
    if vres is not None:
        v0, v1, v2, v_first = vres
        args += [v0, v1, v2, v_first]
        specs += [_full(v0.shape), _full(v1.shape), _full(v2.shape), row]
    return pl.pallas_call(
        functools.partial(_rwkv_proj_kernel, has_vres=vres is not None),
        grid=(t // tm,),
        in_specs=specs,
        out_specs=[row] * 7,
        out_shape=[jax.ShapeDtypeStruct((t, d), F32)] * 7,
        compiler_params=_params(1),
        name="rwkv_proj",
    )(*args)


def _rwkv_scan_kernel(r_ref, ld_ref, k_ref, v_ref, na_ref, nb_ref, y_ref, s_ref):
    c = pl.program_id(0)
    n_pairs = r_ref.shape[1] // LANES
    L = CHUNK
    assert 2 * L == LANES and L == HEAD_DIM

    @pl.when(c == 0)
    def _():
        s_ref[...] = jnp.zeros_like(s_ref)

    ri = lax.broadcasted_iota(jnp.int32, (2 * L, 2 * L), 0)
    ci = lax.broadcasted_iota(jnp.int32, (2 * L, 2 * L), 1)
    same_head = (ri < L) == (ci < L)
    strict = (ci & (L - 1)) < (ri & (L - 1))
    incl = (ci & (L - 1)) <= (ri & (L - 1))
    lane = lax.broadcasted_iota(jnp.int32, (L, LANES), 1)
    head0 = lane < HEAD_DIM
    lane2 = lax.broadcasted_iota(jnp.int32, (2 * L, LANES), 1)
    tri = (lax.broadcasted_iota(jnp.int32, (L, L), 0)
           >= lax.broadcasted_iota(jnp.int32, (L, L), 1)).astype(BF16)

    pairs = range(n_pairs)
    lanes_of = lambda a, p: a[:, p * LANES:(p + 1) * LANES]
    stack2 = lambda a, b: jnp.concatenate([a, b], axis=0)

    r = r_ref[...]
    ld = ld_ref[...]
    k = k_ref[...]
    v = v_ref[...]
    na = na_ref[...]
    nb = nb_ref[...]
    l1 = ld.astype(BF16)
    rem = ld - l1.astype(F32)
    l2 = rem.astype(BF16)
    l3 = (rem - l2.astype(F32)).astype(BF16)
    tri_dot = lambda a: jnp.dot(tri, a, preferred_element_type=F32)
    cum = tri_dot(l1) + (tri_dot(l2) + tri_dot(l3))
    cum_last = cum[L - 1:L, :]
    w_inv = jnp.exp(-cum)
    w_rel = jnp.exp(cum_last - cum)
    w_last = jnp.exp(cum_last)
    x = stack2(r * jnp.exp(cum), na * jnp.exp(cum - ld)).astype(BF16)
    bk = stack2(nb * w_inv, k * w_inv).astype(BF16)
    kb = stack2(k * w_inv, nb * w_inv).astype(BF16)
    keys = stack2(nb * w_rel, k * w_rel).astype(BF16)
    vv = stack2(v, v).astype(BF16)
    zero = jnp.zeros((2 * L, LANES), BF16)
    first_head = lane2 < HEAD_DIM

    s_old = [s_ref[p] for p in pairs]
    xp = [lanes_of(x, p) for p in pairs]
    g0 = [_mm_nt(jnp.where(first_head, xp[p], zero), lanes_of(bk, p)) for p in pairs]
    g1 = [_mm_nt(jnp.where(first_head, zero, xp[p]), lanes_of(kb, p)) for p in pairs]
    xs = [_mm_nt(xp[p], s_old[p]) for p in pairs]
    qs = [stack2(g0[p][L:], g1[p][L:]) for p in pairs]
    ms = [stack2(g0[p][:L], g1[p][:L]) for p in pairs]
    other_head = jnp.logical_not(same_head)
    pw = [jnp.where(same_head & strict, qs[p], 0.0) for p in pairs]
    u = [stack2(xs[p][L:], xs[p][L:])
         + _mm(jnp.where(other_head & strict, qs[p], 0.0), lanes_of(vv, p)) for p in pairs]
    for _ in range(5):
        both = [_mm(pw[p], jnp.concatenate([u[p], pw[p]], axis=1)) for p in pairs]
        u = [u[p] + both[p][:, :LANES] for p in pairs]
        pw = [both[p][:, LANES:] for p in pairs]
    tail = [_mm(pw[p], u[p]) for p in pairs]
    u = [u[p] + tail[p] for p in pairs]
    u = [jnp.where(head0, u[p][:L], u[p][L:]) for p in pairs]

    for p in pairs:
        m_uv = jnp.concatenate([jnp.where(same_head & incl, ms[p], 0.0),
                                jnp.where(other_head & incl, ms[p], 0.0)], axis=1)
        ub = u[p].astype(BF16)
        y = stack2(xs[p][:L], xs[p][:L]) + _mm(m_uv, jnp.concatenate(
            [ub, ub, lanes_of(vv, p)], axis=0))
        y_ref[:, p * LANES:(p + 1) * LANES] = jnp.where(head0, y[:L], y[L:])
    for p in pairs:
        vals = stack2(u[p], lanes_of(v, p))
        upd = _mm(vals.T, lanes_of(keys, p))
        s_ref[p] = jnp.where(same_head, s_old[p] * lanes_of(w_last, p) + upd, 0.0)


def _rwkv_scan(r, ld, k, v, na, nb):
    t, d = r.shape
    blk = pl.BlockSpec((CHUNK, d), lambda c: (c, 0))
    return pl.pallas_call(
        _rwkv_scan_kernel,
        grid=(t // CHUNK,),
        in_specs=[blk] * 6,
        out_specs=blk,
        out_shape=jax.ShapeDtypeStruct((t, d), F32),
        scratch_shapes=[pltpu.VMEM((d // LANES, LANES, LANES), F32)],
        compiler_params=_params(1, parallel=False),
        name="rwkv_scan",
    )(r, ld, k, v, na, nb)


def _rwkv_post_kernel(y_ref, r_ref, k_ref, v_ref, g_ref, x_ref, lnw_ref, lnb_ref, rk_ref,
                      wo_ref, e_ref, o_ref):
    e = e_ref[...]
    y = y_ref[...]
    inv_n = 1.0 / HEAD_DIM
    yc = y - _mm_exact_rhs(y, e) * inv_n
    var = _mm_exact_rhs(yc * yc, e) * inv_n
    yn = yc * lax.rsqrt(var + GN_EPS) * lnw_ref[...] + lnb_ref[...]
    v = v_ref[...]
    bonus = _mm_exact_rhs(r_ref[...] * k_ref[...] * rk_ref[...], e) * v
    z = (yn + bonus) * g_ref[...]
    o_ref[...] = x_ref[...] + _mm(z, wo_ref[...])


def _rwkv_post(y, r, k, v, g, x, ln_w, ln_b, r_k, w_o, e_seg):
    t, d = x.shape
    tm = TOKEN_TILE
    row = pl.BlockSpec((tm, d), lambda i: (i, 0))
    return pl.pallas_call(
        _rwkv_post_kernel,
        grid=(t // tm,),
        in_specs=[row] * 6 + [_full(ln_w.shape), _full(ln_b.shape), _full(r_k.shape),
                              _resident(w_o.shape), _resident(e_seg.shape)],
        out_specs=row,
        out_shape=jax.ShapeDtypeStruct((t, d), F32),
        compiler_params=_params(1),
        name="rwkv_post",
    )(y, r, k, v, g, x, ln_w, ln_b, r_k, w_o, e_seg)


def _head_rms(y, e, gain):
    ms = _mm_exact_rhs(y * y, e) * (1.0 / HEAD_DIM)
    return y * lax.rsqrt(ms + RMS_EPS) * gain


def _kv_proj_kernel(x_ref, gn_ref, w_ref, gain_ref, e_ref, o_ref, hn_ref):
    j = pl.program_id(1)
    n_groups = len(DIL_GROUPS)

    @pl.when(j == 0)
    def _():
        hn_ref[...] = _rms(x_ref[...], gn_ref[...]).astype(BF16)

    y = jnp.dot(hn_ref[...], w_ref[...], preferred_element_type=F32)

    @pl.when(j < n_groups)
    def _():
        o_ref[...] = _head_rms(y, e_ref[...], gain_ref[...]).astype(BF16)

    @pl.when(j >= n_groups)
    def _():
        o_ref[...] = y.astype(BF16)


def _kv_proj(x, gn, w_kv, gains, e_seg):
    t, d = x.shape
    tm = TOKEN_TILE
    n_blk = w_kv.shape[1] // d
    return pl.pallas_call(
        _kv_proj_kernel,
        grid=(t // tm, n_blk),
        in_specs=[
            pl.BlockSpec((tm, d), lambda i, j: (i, 0)),
            _full(gn.shape),
            pl.BlockSpec((d, d), lambda i, j: (0, j)),
            pl.BlockSpec((None, 1, d), lambda i, j: (jnp.minimum(j, len(DIL_GROUPS) - 1), 0, 0)),
            _resident(e_seg.shape),
        ],
        out_specs=pl.BlockSpec((None, tm, d), lambda i, j: (j, i, 0)),
        out_shape=jax.ShapeDtypeStruct((n_blk, t, d), BF16),
        scratch_shapes=[pltpu.VMEM((tm, d), BF16)],
        compiler_params=pltpu.CompilerParams(dimension_semantics=("parallel", "arbitrary"),
                                             vmem_limit_bytes=VMEM_LIMIT),
        name="kv_proj",
    )(x, gn, w_kv, gains, e_seg)


def _q_proj_kernel(x_ref, gn_ref, w_ref, gain_ref, e_ref, o_ref, hn_ref):
    @pl.when(pl.program_id(1) == 0)
    def _():
        hn_ref[...] = _rms(x_ref[...], gn_ref[...]).astype(BF16)

    y = jnp.dot(hn_ref[...], w_ref[...], preferred_element_type=F32)
    o_ref[...] = (_head_rms(y, e_ref[...], gain_ref[...]) * HEAD_DIM ** -0.5).astype(BF16)


def _q_proj(x, gn, w_q, gains, e_seg):
    t, d = x.shape
    tm = TOKEN_TILE
    n_blk = w_q.shape[1] // d
    return pl.pallas_call(
        _q_proj_kernel,
        grid=(t // tm, n_blk),
        in_specs=[
            pl.BlockSpec((tm, d), lambda i, j: (i, 0)),
            _full(gn.shape),
            pl.BlockSpec((d, d), lambda i, j: (0, j)),
            pl.BlockSpec((None, 1, d), lambda i, j: (j, 0, 0)),
            _resident(e_seg.shape),
        ],
        out_specs=pl.BlockSpec((None, tm, d), lambda i, j: (j, i, 0)),
        out_shape=jax.ShapeDtypeStruct((n_blk, t, d), BF16),
        scratch_shapes=[pltpu.VMEM((tm, d), BF16)],
        compiler_params=pltpu.CompilerParams(dimension_semantics=("parallel", "arbitrary"),
                                             vmem_limit_bytes=VMEM_LIMIT),
        name="q_proj",
    )(x, gn, w_q, gains, e_seg)


def _dil_attn_kernel(q_ref, kp_ref, kc_ref, vp_ref, vc_ref, o_ref, m_ref, l_ref):
    n = pl.program_id(1)
    blk = ATT_BLK
    n_pairs = q_ref.shape[1] // LANES
    qi = lax.broadcasted_iota(jnp.int32, (blk, 2 * blk), 0)
    kj = lax.broadcasted_iota(jnp.int32, (blk, 2 * blk), 1)
    valid = (kj >= qi) & (kj <= qi + blk) & ((n > 0) | (kj >= blk))
    lane = lax.broadcasted_iota(jnp.int32, (blk, LANES), 1)
    head0 = lane < HEAD_DIM
    dn_nt = (((1,), (1,)), ((), ()))

    m_acc = jnp.zeros((blk, LANES), F32)
    l_acc = jnp.ones((blk, LANES), F32)
    zero = jnp.zeros((blk, LANES), BF16)
    heads = [(p, hh) for p in range(n_pairs) for hh in range(2)]
    lanes_of = lambda ref, p: ref[:, p * LANES:(p + 1) * LANES]
    scores = []
    for p, hh in heads:
        qp = lanes_of(q_ref, p)
        qm = jnp.where(head0, qp, zero) if hh == 0 else jnp.where(head0, zero, qp)
        kcat = jnp.concatenate([lanes_of(kp_ref, p), lanes_of(kc_ref, p)], axis=0)
        scores.append(lax.dot_general(qm, kcat, dn_nt, preferred_element_type=F32))
    probs = []
    for (p, hh), s in zip(heads, scores):
        s = jnp.where(valid, s, NEG_INF)
        mx = jnp.max(s, axis=-1, keepdims=True)
        pe = jnp.exp(s - mx)
        hit = lane == 2 * p + hh
        m_acc = jnp.where(hit, mx, m_acc)
        l_acc = jnp.where(hit, jnp.sum(pe, axis=-1, keepdims=True), l_acc)
        probs.append(pe.astype(BF16))
    for p in range(n_pairs):
        vcat = jnp.concatenate([lanes_of(vp_ref, p), lanes_of(vc_ref, p)], axis=0)
        o0 = jnp.dot(probs[2 * p], vcat, preferred_element_type=F32)
        o1 = jnp.dot(probs[2 * p + 1], vcat, preferred_element_type=F32)
        o_ref[:, p * LANES:(p + 1) * LANES] = jnp.where(head0, o0, o1)
    m_ref[...] = m_acc
    l_ref[...] = l_acc


def _dil_attn(q, kv, group):
    n_groups, t, d = q.shape
    _, dil = DIL_GROUPS[group]
    m_rows = t // dil
    qv = q.reshape(n_groups, m_rows, dil * d)
    kvv = kv.reshape(2 * n_groups, m_rows, dil * d)
    blk = ATT_BLK
    cur = lambda a: pl.BlockSpec((None, blk, d), lambda r, n: (a, n, r))
    prev = lambda a: pl.BlockSpec((None, blk, d), lambda r, n: (a, jnp.maximum(n - 1, 0), r))
    o, m, l = pl.pallas_call(
        _dil_attn_kernel,
        grid=(dil, m_rows // blk),
        in_specs=[cur(group), prev(group), cur(group), prev(n_groups + group), cur(n_groups + group)],
        out_specs=[pl.BlockSpec((blk, d), lambda r, n: (n, r)),
                   pl.BlockSpec((blk, LANES), lambda r, n: (n, r)),
                   pl.BlockSpec((blk, LANES), lambda r, n: (n, r))],
        out_shape=[jax.ShapeDtypeStruct((m_rows, dil * d), F32),
                   jax.ShapeDtypeStruct((m_rows, dil * LANES), F32),
                   jax.ShapeDtypeStruct((m_rows, dil * LANES), F32)],
        compiler_params=_params(2),
        name="dil_attn",
    )(qv, kvv, kvv, kvv, kvv)
    return o.reshape(t, d), m.reshape(t, LANES), l.reshape(t, LANES)


def _attn_combine_kernel(o0, o1, o2, m0, m1, m2, l0, l1, l2, x_ref, wo_ref, ex_ref, out_ref):
    ms = [m0[...], m1[...], m2[...]]
    top = jnp.maximum(jnp.maximum(ms[0], ms[1]), ms[2])
    es = [jnp.exp(mg - top) for mg in ms]
    den = es[0] * l0[...] + es[1] * l1[...] + es[2] * l2[...]
    ex = ex_ref[...]
    acc = None
    for eg, og in zip(es, (o0, o1, o2)):
        coef = _mm_exact_rhs(eg / den, ex)
        term = coef * og[...]
        acc = term if acc is None else acc + term
    out_ref[...] = x_ref[...] + _mm(acc, wo_ref[...])


def _attn_combine(parts, x, w_o, expand):
    t, d = x.shape
    tm = TOKEN_TILE
    row = pl.BlockSpec((tm, d), lambda i: (i, 0))
    stat = pl.BlockSpec((tm, LANES), lambda i: (i, 0))
    os_, ms_, ls_ = zip(*parts)
    return pl.pallas_call(
        _attn_combine_kernel,
        grid=(t // tm,),
        in_specs=[row] * 3 + [stat] * 6 + [row, _resident(w_o.shape), _full(expand.shape)],
        out_specs=row,
        out_shape=jax.ShapeDtypeStruct((t, d), F32),
        compiler_params=_params(1),
        name="attn_combine",
    )(*os_, *ms_, *ls_, x, w_o, expand)


def kernel(x, ffn_norm, ffn_w_in, ffn_w_out, mix_norm, rwkv_mu, rwkv_w_rkv, rwkv_w0, rwkv_w1, rwkv_w2, rwkv_a0, rwkv_a1, rwkv_a2, rwkv_v0, rwkv_v1, rwkv_v2, rwkv_g1, rwkv_g2, rwkv_k_k, rwkv_k_a, rwkv_r_k, rwkv_ln_w, rwkv_ln_b, rwkv_w_o, kv_norm, w_kv, k_norm, attn_w_q, q_norm, attn_w_o):
    bsz, t, d = x.shape
    assert bsz == 1 and d % LANES == 0 and t % max(w for w, _ in DIL_GROUPS) == 0
    depth = ffn_norm.shape[0]
    n_a = rwkv_mu.shape[0]
    n_groups = len(DIL_GROUPS)
    row = lambda p: p.reshape(1, -1)

    ffn_g = ffn_norm.reshape(2 * depth, 1, d)
    ffn_wi = ffn_w_in.astype(BF16).reshape(2 * depth, d, -1)
    ffn_wo = ffn_w_out.astype(BF16).reshape(2 * depth, -1, d)
    head_of_lane = jnp.arange(d) // HEAD_DIM
    e_seg = (head_of_lane[:, None] == head_of_lane[None, :]).astype(BF16)
    expand = (jnp.arange(LANES)[:, None] == head_of_lane[None, :]).astype(BF16)

    xs = x.reshape(t, d)
    v_first = None
    kv = None
    for l in range(depth):
        xs = _ffn(xs, ffn_g, ffn_wi, ffn_wo, 2 * l)
        if l < n_a:
            vres = None
            if l > 0:
                vres = (row(rwkv_v0[l - 1]), rwkv_v1[l - 1].astype(BF16),
                        rwkv_v2[l - 1].astype(BF16), v_first)
            r, ld, k, v, na, nb, g = _rwkv_proj(
                xs, row(mix_norm[l]), rwkv_mu[l], rwkv_w_rkv[l].astype(BF16), row(rwkv_w0[l]),
                rwkv_w1[l].astype(BF16), rwkv_w2[l].astype(BF16), row(rwkv_a0[l]),
                rwkv_a1[l].astype(BF16), rwkv_a2[l].astype(BF16), rwkv_g1[l].astype(BF16),
                rwkv_g2[l].astype(BF16), row(rwkv_k_k[l]), row(rwkv_k_a[l]), e_seg, vres)
            if l == 0:
                v_first = v
            y = _rwkv_scan(r, ld, k, v, na, nb)
            xs = _rwkv_post(y, r, k, v, g, xs, row(rwkv_ln_w[l]), row(rwkv_ln_b[l]),
                            row(rwkv_r_k[l]), rwkv_w_o[l].astype(BF16), e_seg)
        else:
            i = l - n_a
            q_gain = jnp.tile(q_norm[i], (1, d // HEAD_DIM)).reshape(n_groups, 1, d)
            q = _q_proj(xs, row(mix_norm[l]), attn_w_q[i].astype(BF16), q_gain, e_seg)
            parts = [_dil_attn(q, kv, grp) for grp in range(n_groups)]
            xs = _attn_combine(parts, xs, attn_w_o[i].astype(BF16), expand)
        xs = _ffn(xs, ffn_g, ffn_wi, ffn_wo, 2 * l + 1)
        if l == n_a - 1:
            k_gain = jnp.tile(k_norm, (1, d // HEAD_DIM)).reshape(n_groups, 1, d)
            kv = _kv_proj(xs, row(kv_norm), w_kv.astype(BF16), k_gain, e_seg)
    return xs.reshape(bsz, t, d)
```

```python
import functools

import jax
import jax.numpy as jnp
from jax import lax
from jax.experimental import pallas as pl
from jax.experimental.pallas import tpu as pltpu

F32 = jnp.float32
BF16 = jnp.bfloat16

RMS_EPS = 1e-6
GN_EPS = 64e-5
HEAD_DIM = 64
LANES = 128
DIL_GROUPS = ((128, 1), (512, 4), (2048, 16))
ATT_BLK = 128
NEG_INF = -1e30
CHUNK = 64
SCAN_BLOCK = 256
TOKEN_TILE = 512
FFN_COL_TILE = 256
VMEM_LIMIT = 56 * 1024 * 1024


def _params(*sem):
    return pltpu.CompilerParams(dimension_semantics=sem, vmem_limit_bytes=VMEM_LIMIT)


def _mm(a, b):
    return jnp.dot(a.astype(BF16), b.astype(BF16), preferred_element_type=F32)


def _mm_nt(a, b):
    return lax.dot_general(a.astype(BF16), b.astype(BF16), (((1,), (1,)), ((), ())),
                           preferred_element_type=F32)


def _mm_exact_rhs(a, b_bf16):
    hi = a.astype(BF16)
    lo = (a - hi.astype(F32)).astype(BF16)
    d = lambda p: jnp.dot(p, b_bf16, preferred_element_type=F32)
    return d(hi) + d(lo)


def _rms(x, g):
    return x * lax.rsqrt(jnp.mean(x * x, axis=-1, keepdims=True) + RMS_EPS) * g


def _sigmoid(x):
    return 1.0 / (1.0 + jnp.exp(-x))


def _full(shape):
    n = len(shape)
    return pl.BlockSpec(shape, lambda *_: (0,) * n)


def _resident(shape):
    n = len(shape)
    return pl.BlockSpec(shape, lambda *_: (0,) * n, pipeline_mode=pl.Buffered(1))


def _ffn_kernel(x_ref, g_ref, win_ref, wout_ref, o_ref, act_ref):
    x = x_ref[...]
    hn = _rms(x, g_ref[...]).astype(BF16)
    d_ff = wout_ref.shape[0]
    for c in range(d_ff // FFN_COL_TILE):
        lo = c * FFN_COL_TILE
        gate = jnp.dot(hn, win_ref[:, lo:lo + FFN_COL_TILE], preferred_element_type=F32)
        up = jnp.dot(hn, win_ref[:, d_ff + lo:d_ff + lo + FFN_COL_TILE],
                     preferred_element_type=F32)
        act_ref[:, lo:lo + FFN_COL_TILE] = (gate * _sigmoid(gate) * up).astype(BF16)
    o_ref[...] = x + 0.5 * jnp.dot(act_ref[...], wout_ref[...], preferred_element_type=F32)


def _ffn(x, norms, w_in, w_out, idx):
    t, d = x.shape
    d_ff = w_out.shape[1]
    tm = TOKEN_TILE
    return pl.pallas_call(
        _ffn_kernel,
        grid=(t // tm,),
        in_specs=[
            pl.BlockSpec((tm, d), lambda i: (i, 0)),
            pl.BlockSpec((None, 1, d), lambda i: (idx, 0, 0)),
            pl.BlockSpec((None, d, 2 * d_ff), lambda i: (idx, 0, 0), pipeline_mode=pl.Buffered(1)),
            pl.BlockSpec((None, d_ff, d), lambda i: (idx, 0, 0), pipeline_mode=pl.Buffered(1)),
        ],
        out_specs=pl.BlockSpec((tm, d), lambda i: (i, 0)),
        out_shape=jax.ShapeDtypeStruct((t, d), F32),
        scratch_shapes=[pltpu.VMEM((tm, d_ff), BF16)],
        compiler_params=_params("parallel"),
        name="ffn",
    )(x, norms, w_in, w_out)


def _head_sum(x, e_red):
    return _mm_exact_rhs(x, e_red)


def _spread(c, e_exp):
    return _mm_exact_rhs(c, e_exp)


def _head_rms(y, e_red, e_exp, gain):
    ms = _head_sum(y * y, e_red) * (1.0 / HEAD_DIM)
    return y * _spread(lax.rsqrt(ms + RMS_EPS), e_exp) * gain


def _rwkv_proj_kernel(*refs, has_vres):
    if has_vres:
        (x_ref, xp_ref, gn_ref, mu_ref, wrkv_ref, w0_ref, w1_ref, w2_ref, a0_ref, a1_ref,
         a2_ref, g1_ref, g2_ref, kk_ref, ka_ref, ered_ref, eexp_ref, v0_ref, v1_ref, v2_ref,
         vf_ref, r_o, ld_o, k_o, v_o, na_o, nb_o, g_o) = refs
    else:
        (x_ref, xp_ref, gn_ref, mu_ref, wrkv_ref, w0_ref, w1_ref, w2_ref, a0_ref, a1_ref,
         a2_ref, g1_ref, g2_ref, kk_ref, ka_ref, ered_ref, eexp_ref,
         r_o, ld_o, k_o, v_o, na_o, nb_o, g_o) = refs
    i = pl.program_id(0)
    gn = gn_ref[...]
    h = _rms(x_ref[...], gn)
    hp = _rms(xp_ref[7:8, :], gn) * jnp.where(i > 0, 1.0, 0.0)
    rows = lax.broadcasted_iota(jnp.int32, h.shape, 0)
    hs = jnp.where(rows == 0, hp, pltpu.roll(h, 1, axis=0))
    xx = hs - h
    mu = mu_ref[...]
    xr, xw, xk, xv, xa, xg = [h + xx * mu[j:j + 1] for j in range(6)]
    r = _mm(xr, wrkv_ref[0])
    k = _mm(xk, wrkv_ref[1])
    v = _mm(xv, wrkv_ref[2])
    w = w0_ref[...] + _mm(jnp.tanh(_mm(xw, w1_ref[...])), w2_ref[...])
    z = -w
    wl = -(jnp.maximum(z, 0.0) + jnp.log(1.0 + jnp.exp(-jnp.abs(z)))) - 0.5
    ld_o[...] = -jnp.exp(wl)
    a = _sigmoid(a0_ref[...] + _mm(_mm(xa, a1_ref[...]), a2_ref[...]))
    g_o[...] = _mm(_sigmoid(_mm(xg, g1_ref[...])), g2_ref[...])
    if has_vres:
        v = v + (vf_ref[...] - v) * _sigmoid(v0_ref[...] + _mm(_mm(xv, v1_ref[...]), v2_ref[...]))
    kkr = k * kk_ref[...]
    norm = jnp.sqrt(_head_sum(kkr * kkr, ered_ref[...]))
    kk = kkr * _spread(1.0 / jnp.maximum(norm, 1e-12), eexp_ref[...])
    r_o[...] = r
    k_o[...] = k * (1.0 + (a - 1.0) * ka_ref[...])
    v_o[...] = v
    na_o[...] = -kk
    nb_o[...] = kk * a


def _rwkv_proj(x, gn, mu, wrkv, w0, w1, w2, a0, a1, a2, g1, g2, k_k, k_a, e_red, e_exp, vres):
    t, d = x.shape
    tm = TOKEN_TILE // 2
    row = pl.BlockSpec((tm, d), lambda i: (i, 0))
    prev = pl.BlockSpec((8, d), lambda i: (jnp.maximum(i * (tm // 8) - 1, 0), 0))
    args = [x, x, gn, mu, wrkv, w0, w1, w2, a0, a1, a2, g1, g2, k_k, k_a, e_red, e_exp]
    specs = [row, prev, _full(gn.shape), _full(mu.shape), _resident(wrkv.shape), _full(w0.shape),
             _full(w1.shape), _full(w2.shape), _full(a0.shape), _full(a1.shape), _full(a2.shape),
             _full(g1.shape), _full(g2.shape), _full(k_k.shape), _full(k_a.shape),
             _full(e_red.shape), _full(e_exp.shape)]
    if vres is not None:
        v0, v1, v2, v_first = vres
        args += [v0, v1, v2, v_first]
        specs += [_full(v0.shape), _full(v1.shape), _full(v2.shape), row]
    return pl.pallas_call(
        functools.partial(_rwkv_proj_kernel, has_vres=vres is not None),
        grid=(t // tm,),
        in_specs=specs,
        out_specs=[row] * 7,
        out_shape=[jax.ShapeDtypeStruct((t, d), F32)] * 7,
        compiler_params=_params("parallel"),
        name="rwkv_proj",
    )(*args)


def _rwkv_scan_kernel(r_ref, ld_ref, k_ref, v_ref, na_ref, nb_ref, y_ref, s_ref):
    n_pairs = r_ref.shape[1] // LANES
    L = CHUNK
    assert 2 * L == LANES and L == HEAD_DIM

    @pl.when(pl.program_id(0) == 0)
    def _():
        s_ref[...] = jnp.zeros_like(s_ref)

    ri = lax.broadcasted_iota(jnp.int32, (2 * L, 2 * L), 0)
    ci = lax.broadcasted_iota(jnp.int32, (2 * L, 2 * L), 1)
    same_head = (ri < L) == (ci < L)
    other_head = jnp.logical_not(same_head)
    strict = (ci & (L - 1)) < (ri & (L - 1))
    incl = (ci & (L - 1)) <= (ri & (L - 1))
    head0 = lax.broadcasted_iota(jnp.int32, (L, LANES), 1) < HEAD_DIM
    first_head = lax.broadcasted_iota(jnp.int32, (2 * L, LANES), 1) < HEAD_DIM
    tri = (lax.broadcasted_iota(jnp.int32, (L, L), 0)
           >= lax.broadcasted_iota(jnp.int32, (L, L), 1)).astype(BF16)
    zero = jnp.zeros((2 * L, LANES), BF16)
    pairs = range(n_pairs)
    lanes_of = lambda a, p: a[:, p * LANES:(p + 1) * LANES]
    stack2 = lambda a, b: jnp.concatenate([a, b], axis=0)

    def chunk_body(j, carry):
        rows = pl.ds(pl.multiple_of(j * L, L), L)
        r = r_ref[rows, :]
        ld = ld_ref[rows, :]
        k = k_ref[rows, :]
        v = v_ref[rows, :]
        na = na_ref[rows, :]
        nb = nb_ref[rows, :]
        l1 = ld.astype(BF16)
        rem = ld - l1.astype(F32)
        l2 = rem.astype(BF16)
        l3 = (rem - l2.astype(F32)).astype(BF16)
        tri_dot = lambda a: jnp.dot(tri, a, preferred_element_type=F32)
        cum = tri_dot(l1) + (tri_dot(l2) + tri_dot(l3))
        cum_last = cum[L - 1:L, :]
        w_inv = jnp.exp(-cum)
        w_rel = jnp.exp(cum_last - cum)
        w_last = jnp.exp(cum_last)
        x = stack2(r * jnp.exp(cum), na * jnp.exp(cum - ld)).astype(BF16)
        bk = stack2(nb * w_inv, k * w_inv).astype(BF16)
        kb = stack2(k * w_inv, nb * w_inv).astype(BF16)
        keys = stack2(nb * w_rel, k * w_rel).astype(BF16)
        vv = stack2(v, v).astype(BF16)

        s_old = [s_ref[p] for p in pairs]
        xp = [lanes_of(x, p) for p in pairs]
        g0 = [_mm_nt(jnp.where(first_head, xp[p], zero), lanes_of(bk, p)) for p in pairs]
        g1 = [_mm_nt(jnp.where(first_head, zero, xp[p]), lanes_of(kb, p)) for p in pairs]
        xs = [_mm_nt(xp[p], s_old[p]) for p in pairs]
        qs = [stack2(g0[p][L:], g1[p][L:]) for p in pairs]
        ms = [stack2(g0[p][:L], g1[p][:L]) for p in pairs]
        pw = [jnp.where(same_head & strict, qs[p], 0.0) for p in pairs]
        u = [stack2(xs[p][L:], xs[p][L:])
             + _mm(jnp.where(other_head & strict, qs[p], 0.0), lanes_of(vv, p)) for p in pairs]
        for _ in range(5):
            both = [_mm(pw[p], jnp.concatenate([u[p], pw[p]], axis=1)) for p in pairs]
            u = [u[p] + both[p][:, :LANES] for p in pairs]
            pw = [both[p][:, LANES:] for p in pairs]
        tail = [_mm(pw[p], u[p]) for p in pairs]
        u = [u[p] + tail[p] for p in pairs]
        u = [jnp.where(head0, u[p][:L], u[p][L:]) for p in pairs]

        for p in pairs:
            m_uv = jnp.concatenate([jnp.where(same_head & incl, ms[p], 0.0),
                                    jnp.where(other_head & incl, ms[p], 0.0)], axis=1)
            ub = u[p].astype(BF16)
            y = stack2(xs[p][:L], xs[p][:L]) + _mm(m_uv, jnp.concatenate(
                [ub, ub, lanes_of(vv, p)], axis=0))
            y_ref[rows, p * LANES:(p + 1) * LANES] = jnp.where(head0, y[:L], y[L:])
        for p in pairs:
            vals = stack2(u[p], lanes_of(v, p))
            upd = _mm(vals.T, lanes_of(keys, p))
            s_ref[p] = jnp.where(same_head, s_old[p] * lanes_of(w_last, p) + upd, 0.0)
        return carry

    lax.fori_loop(0, r_ref.shape[0] // L, chunk_body, 0)


def _rwkv_scan(r, ld, k, v, na, nb):
    t, d = r.shape
    blk = pl.BlockSpec((SCAN_BLOCK, d), lambda c: (c, 0))
    return pl.pallas_call(
        _rwkv_scan_kernel,
        grid=(t // SCAN_BLOCK,),
        in_specs=[blk] * 6,
        out_specs=blk,
        out_shape=jax.ShapeDtypeStruct((t, d), F32),
        scratch_shapes=[pltpu.VMEM((d // LANES, LANES, LANES), F32)],
        compiler_params=_params("arbitrary"),
        name="rwkv_scan",
    )(r, ld, k, v, na, nb)


def _rwkv_post_kernel(y_ref, r_ref, k_ref, v_ref, g_ref, x_ref, lnw_ref, lnb_ref, rk_ref,
                      wo_ref, ered_ref, eexp_ref, o_ref):
    e_red = ered_ref[...]
    e_exp = eexp_ref[...]
    y = y_ref[...]
    inv_n = 1.0 / HEAD_DIM
    yc = y - _spread(_head_sum(y, e_red) * inv_n, e_exp)
    var = _head_sum(yc * yc, e_red) * inv_n
    yn = yc * _spread(lax.rsqrt(var + GN_EPS), e_exp) * lnw_ref[...] + lnb_ref[...]
    bonus = _spread(_head_sum(r_ref[...] * k_ref[...] * rk_ref[...], e_red), e_exp) * v_ref[...]
    z = (yn + bonus) * g_ref[...]
    o_ref[...] = x_ref[...] + _mm(z, wo_ref[...])


def _rwkv_post(y, r, k, v, g, x, ln_w, ln_b, r_k, w_o, e_red, e_exp):
    t, d = x.shape
    tm = TOKEN_TILE
    row = pl.BlockSpec((tm, d), lambda i: (i, 0))
    return pl.pallas_call(
        _rwkv_post_kernel,
        grid=(t // tm,),
        in_specs=[row] * 6 + [_full(ln_w.shape), _full(ln_b.shape), _full(r_k.shape),
                              _resident(w_o.shape), _full(e_red.shape), _full(e_exp.shape)],
        out_specs=row,
        out_shape=jax.ShapeDtypeStruct((t, d), F32),
        compiler_params=_params("parallel"),
        name="rwkv_post",
    )(y, r, k, v, g, x, ln_w, ln_b, r_k, w_o, e_red, e_exp)


def _store_dilated(dst_ref, y, stage_ref, dil):
    rows, d = y.shape
    if dil == 1:
        dst_ref[...] = y.astype(dst_ref.dtype)
        return
    n_tiles = d // LANES
    for c in range(n_tiles):
        stage_ref[c] = y[:, c * LANES:(c + 1) * LANES]
    for r in range(dil):
        for c in range(n_tiles):
            lo = r * d + c * LANES
            dst_ref[:, lo:lo + LANES] = (
                stage_ref[c, pl.ds(r, rows // dil, stride=dil), :].astype(dst_ref.dtype))


def _kv_proj_kernel(x_ref, gn_ref, w_ref, gain_ref, ered_ref, eexp_ref, o0_ref, o1_ref, o2_ref,
                    hn_ref, stage_ref):
    j = pl.program_id(1)
    n_groups = len(DIL_GROUPS)

    @pl.when(j == 0)
    def _():
        hn_ref[...] = _rms(x_ref[...], gn_ref[...]).astype(BF16)

    y = jnp.dot(hn_ref[...], w_ref[...], preferred_element_type=F32)
    outs = (o0_ref, o1_ref, o2_ref)
    for jj in range(2 * n_groups):
        is_value, grp = divmod(jj, n_groups)

        @pl.when(j == jj)
        def _(is_value=is_value, grp=grp):
            val = y if is_value else _head_rms(y, ered_ref[...], eexp_ref[...], gain_ref[...])
            _store_dilated(outs[grp].at[is_value], val, stage_ref, DIL_GROUPS[grp][1])


def _kv_proj(x, gn, w_kv, gains, e_red, e_exp):
    t, d = x.shape
    tm = TOKEN_TILE
    n_blk = w_kv.shape[1] // d
    n_groups = len(DIL_GROUPS)
    return pl.pallas_call(
        _kv_proj_kernel,
        grid=(t // tm, n_blk),
        in_specs=[
            pl.BlockSpec((tm, d), lambda i, j: (i, 0)),
            _full(gn.shape),
            pl.BlockSpec((d, d), lambda i, j: (0, j)),
            pl.BlockSpec((None, 1, d), lambda i, j: (jnp.minimum(j, n_groups - 1), 0, 0)),
            _full(e_red.shape), _full(e_exp.shape),
        ],
        out_specs=[pl.BlockSpec((2, tm // dil, dil * d), lambda i, j: (0, i, 0))
                   for _, dil in DIL_GROUPS],
        out_shape=[jax.ShapeDtypeStruct((2, t // dil, dil * d), BF16) for _, dil in DIL_GROUPS],
        scratch_shapes=[pltpu.VMEM((tm, d), BF16), pltpu.VMEM((d // LANES, tm, LANES), F32)],
        compiler_params=_params("parallel", "arbitrary"),
        name="kv_proj",
    )(x, gn, w_kv, gains, e_red, e_exp)


def _q_proj_kernel(x_ref, gn_ref, w_ref, gain_ref, ered_ref, eexp_ref, o0_ref, o1_ref, o2_ref,
                   hn_ref, stage_ref):
    j = pl.program_id(1)

    @pl.when(j == 0)
    def _():
        hn_ref[...] = _rms(x_ref[...], gn_ref[...]).astype(BF16)

    y = jnp.dot(hn_ref[...], w_ref[...], preferred_element_type=F32)
    y = _head_rms(y, ered_ref[...], eexp_ref[...], gain_ref[...]) * HEAD_DIM ** -0.5
    for grp, out_ref in enumerate((o0_ref, o1_ref, o2_ref)):

        @pl.when(j == grp)
        def _(grp=grp, out_ref=out_ref):
            _store_dilated(out_ref, y, stage_ref, DIL_GROUPS[grp][1])


def _q_proj(x, gn, w_q, gains, e_red, e_exp):
    t, d = x.shape
    tm = TOKEN_TILE
    n_blk = w_q.shape[1] // d
    return pl.pallas_call(
        _q_proj_kernel,
        grid=(t // tm, n_blk),
        in_specs=[
            pl.BlockSpec((tm, d), lambda i, j: (i, 0)),
            _full(gn.shape),
            pl.BlockSpec((d, d), lambda i, j: (0, j)),
            pl.BlockSpec((None, 1, d), lambda i, j: (j, 0, 0)),
            _full(e_red.shape), _full(e_exp.shape),
        ],
        out_specs=[pl.BlockSpec((tm // dil, dil * d), lambda i, j: (i, 0)) for _, dil in DIL_GROUPS],
        out_shape=[jax.ShapeDtypeStruct((t // dil, dil * d), BF16) for _, dil in DIL_GROUPS],
        scratch_shapes=[pltpu.VMEM((tm, d), BF16), pltpu.VMEM((d // LANES, tm, LANES), F32)],
        compiler_params=_params("parallel", "arbitrary"),
        name="q_proj",
    )(x, gn, w_q, gains, e_red, e_exp)


def _dil_attn_kernel(q_ref, kp_ref, kc_ref, vp_ref, vc_ref, o_ref, m_ref, l_ref):
    n = pl.program_id(1)
    blk = ATT_BLK
    n_pairs = q_ref.shape[1] // LANES
    qi = lax.broadcasted_iota(jnp.int32, (blk, 2 * blk), 0)
    kj = lax.broadcasted_iota(jnp.int32, (blk, 2 * blk), 1)
    valid = (kj >= qi) & (kj <= qi + blk) & ((n > 0) | (kj >= blk))
    lane = lax.broadcasted_iota(jnp.int32, (blk, LANES), 1)
    head0 = lane < HEAD_DIM
    dn_nt = (((1,), (1,)), ((), ()))

    m_acc = jnp.zeros((blk, LANES), F32)
    l_acc = jnp.ones((blk, LANES), F32)
    zero = jnp.zeros((blk, LANES), BF16)
    heads = [(p, hh) for p in range(n_pairs) for hh in range(2)]
    lanes_of = lambda ref, p: ref[:, p * LANES:(p + 1) * LANES]
    scores = []
    for p, hh in heads:
        qp = lanes_of(q_ref, p)
        qm = jnp.where(head0, qp, zero) if hh == 0 else jnp.where(head0, zero, qp)
        kcat = jnp.concatenate([lanes_of(kp_ref, p), lanes_of(kc_ref, p)], axis=0)
        scores.append(lax.dot_general(qm, kcat, dn_nt, preferred_element_type=F32))
    probs = []
    for (p, hh), s in zip(heads, scores):
        s = jnp.where(valid, s, NEG_INF)
        mx = jnp.max(s, axis=-1, keepdims=True)
        pe = jnp.exp(s - mx)
        hit = lane == 2 * p + hh
        m_acc = jnp.where(hit, mx, m_acc)
        l_acc = jnp.where(hit, jnp.sum(pe, axis=-1, keepdims=True), l_acc)
        probs.append(pe.astype(BF16))
    for p in range(n_pairs):
        vcat = jnp.concatenate([lanes_of(vp_ref, p), lanes_of(vc_ref, p)], axis=0)
        o0 = jnp.dot(probs[2 * p], vcat, preferred_element_type=F32)
        o1 = jnp.dot(probs[2 * p + 1], vcat, preferred_element_type=F32)
        o_ref[:, p * LANES:(p + 1) * LANES] = jnp.where(head0, o0, o1)
    m_ref[...] = m_acc
    l_ref[...] = l_acc


def _dil_attn(q, kv, dil):
    m_rows, wide = q.shape
    d = wide // dil
    blk = ATT_BLK
    cur = lambda a: pl.BlockSpec((None, blk, d), lambda r, n: (a, n, r))
    prev = lambda a: pl.BlockSpec((None, blk, d), lambda r, n: (a, jnp.maximum(n - 1, 0), r))
    return pl.pallas_call(
        _dil_attn_kernel,
        grid=(dil, m_rows // blk),
        in_specs=[pl.BlockSpec((blk, d), lambda r, n: (n, r)), prev(0), cur(0), prev(1), cur(1)],
        out_specs=[pl.BlockSpec((blk, d), lambda r, n: (n, r)),
                   pl.BlockSpec((blk, LANES), lambda r, n: (n, r)),
                   pl.BlockSpec((blk, LANES), lambda r, n: (n, r))],
        out_shape=[jax.ShapeDtypeStruct((m_rows, dil * d), F32),
                   jax.ShapeDtypeStruct((m_rows, dil * LANES), F32),
                   jax.ShapeDtypeStruct((m_rows, dil * LANES), F32)],
        compiler_params=_params("parallel", "parallel"),
        name="dil_attn",
    )(q, kv, kv, kv, kv)


def _load_dilated(src_ref, stage_ref, dil):
    if dil == 1:
        return src_ref[...]
    sub = src_ref.shape[0]
    n_tiles = src_ref.shape[1] // dil // LANES
    for r in range(dil):
        for c in range(n_tiles):
            lo = (r * n_tiles + c) * LANES
            stage_ref[c, pl.ds(r, sub, stride=dil), :] = src_ref[:, lo:lo + LANES]
    return jnp.concatenate([stage_ref[c] for c in range(n_tiles)], axis=1)


def _attn_combine_kernel(o0, o1, o2, m0, m1, m2, l0, l1, l2, x_ref, wo_ref, eexp_ref, out_ref,
                         os1, os2, ms1, ms2, ls1, ls2):
    dils = [dil for _, dil in DIL_GROUPS]
    ms = [_load_dilated(ref, st, dil) for ref, st, dil in zip((m0, m1, m2), (None, ms1, ms2), dils)]
    ls = [_load_dilated(ref, st, dil) for ref, st, dil in zip((l0, l1, l2), (None, ls1, ls2), dils)]
    top = jnp.maximum(jnp.maximum(ms[0], ms[1]), ms[2])
    es = [jnp.exp(mg - top) for mg in ms]
    den = es[0] * ls[0] + es[1] * ls[1] + es[2] * ls[2]
    e_exp = eexp_ref[...]
    acc = None
    for eg, ref, st, dil in zip(es, (o0, o1, o2), (None, os1, os2), dils):
        term = _spread(eg / den, e_exp) * _load_dilated(ref, st, dil)
        acc = term if acc is None else acc + term
    out_ref[...] = x_ref[...] + _mm(acc, wo_ref[...])


def _attn_combine(parts, x, w_o, e_exp):
    t, d = x.shape
    tm = TOKEN_TILE
    row = pl.BlockSpec((tm, d), lambda i: (i, 0))
    dils = [dil for _, dil in DIL_GROUPS]
    o_specs = [pl.BlockSpec((tm // dil, dil * d), lambda i: (i, 0)) for dil in dils]
    s_specs = [pl.BlockSpec((tm // dil, dil * LANES), lambda i: (i, 0)) for dil in dils]
    os_, ms_, ls_ = zip(*parts)
    return pl.pallas_call(
        _attn_combine_kernel,
        grid=(t // tm,),
        in_specs=o_specs + s_specs + s_specs + [row, _resident(w_o.shape), _full(e_exp.shape)],
        out_specs=row,
        out_shape=jax.ShapeDtypeStruct((t, d), F32),
        scratch_shapes=([pltpu.VMEM((d // LANES, tm, LANES), F32)] * 2
                        + [pltpu.VMEM((1, tm, LANES), F32)] * 4),
        compiler_params=_params("parallel"),
        name="attn_combine",
    )(*os_, *ms_, *ls_, x, w_o, e_exp)


def kernel(x, ffn_norm, ffn_w_in, ffn_w_out, mix_norm, rwkv_mu, rwkv_w_rkv, rwkv_w0, rwkv_w1, rwkv_w2, rwkv_a0, rwkv_a1, rwkv_a2, rwkv_v0, rwkv_v1, rwkv_v2, rwkv_g1, rwkv_g2, rwkv_k_k, rwkv_k_a, rwkv_r_k, rwkv_ln_w, rwkv_ln_b, rwkv_w_o, kv_norm, w_kv, k_norm, attn_w_q, q_norm, attn_w_o):
    bsz, t, d = x.shape
    assert bsz == 1 and d % LANES == 0 and t % max(w for w, _ in DIL_GROUPS) == 0
    assert all(w // dil == ATT_BLK for w, dil in DIL_GROUPS)
    depth = ffn_norm.shape[0]
    n_a = rwkv_mu.shape[0]
    n_groups = len(DIL_GROUPS)
    row = lambda p: p.reshape(1, -1)

    ffn_g = ffn_norm.reshape(2 * depth, 1, d)
    ffn_wi = ffn_w_in.astype(BF16).reshape(2 * depth, d, -1)
    ffn_wo = ffn_w_out.astype(BF16).reshape(2 * depth, -1, d)
    head_of_lane = jnp.arange(d) // HEAD_DIM
    e_exp = (jnp.arange(LANES)[:, None] == head_of_lane[None, :]).astype(BF16)
    e_red = e_exp.T

    xs = x.reshape(t, d)
    v_first = None
    kvs = None
    for l in range(depth):
        xs = _ffn(xs, ffn_g, ffn_wi, ffn_wo, 2 * l)
        if l < n_a:
            vres = None
            if l > 0:
                vres = (row(rwkv_v0[l - 1]), rwkv_v1[l - 1].astype(BF16),
                        rwkv_v2[l - 1].astype(BF16), v_first)
            r, ld, k, v, na, nb, g = _rwkv_proj(
                xs, row(mix_norm[l]), rwkv_mu[l], rwkv_w_rkv[l].astype(BF16), row(rwkv_w0[l]),
                rwkv_w1[l].astype(BF16), rwkv_w2[l].astype(BF16), row(rwkv_a0[l]),
                rwkv_a1[l].astype(BF16), rwkv_a2[l].astype(BF16), rwkv_g1[l].astype(BF16),
                rwkv_g2[l].astype(BF16), row(rwkv_k_k[l]), row(rwkv_k_a[l]), e_red, e_exp, vres)
            if l == 0:
                v_first = v
            y = _rwkv_scan(r, ld, k, v, na, nb)
            xs = _rwkv_post(y, r, k, v, g, xs, row(rwkv_ln_w[l]), row(rwkv_ln_b[l]),
                            row(rwkv_r_k[l]), rwkv_w_o[l].astype(BF16), e_red, e_exp)
        else:
            i = l - n_a
            q_gain = jnp.tile(q_norm[i], (1, d // HEAD_DIM)).reshape(n_groups, 1, d)
            qs = _q_proj(xs, row(mix_norm[l]), attn_w_q[i].astype(BF16), q_gain, e_red, e_exp)
            parts = [_dil_attn(qs[grp], kvs[grp], DIL_GROUPS[grp][1]) for grp in range(n_groups)]
            xs = _attn_combine(parts, xs, attn_w_o[i].astype(BF16), e_exp)
        xs = _ffn(xs, ffn_g, ffn_wi, ffn_wo, 2 * l + 1)
        if l == n_a - 1:
            k_gain = jnp.tile(k_norm, (1, d // HEAD_DIM)).reshape(n_groups, 1, d)
            kvs = _kv_proj(xs, row(kv_norm), w_kv.astype(BF16), k_gain, e_red, e_exp)
    return xs.reshape(bsz, t, d)
```

```python
import functools

import jax
import jax.numpy as jnp
from jax import lax
from jax.experimental import pallas as pl
from jax.experimental.pallas import tpu as pltpu

F32 = jnp.float32
BF16 = jnp.bfloat16

RMS_EPS = 1e-6
GN_EPS = 64e-5
HEAD_DIM = 64
LANES = 128
DIL_GROUPS = ((128, 1), (512, 4), (2048, 16))
ATT_BLK = 128
ATT_SUB = 2
NEG_INF = -1e30
CHUNK = 64
SCAN_BLOCK = 256
TOKEN_TILE = 512
FFN_COL_TILE = 256
VMEM_LIMIT = 56 * 1024 * 1024


def _params(*sem):
    return pltpu.CompilerParams(dimension_semantics=sem, vmem_limit_bytes=VMEM_LIMIT)


def _mm(a, b):
    return jnp.dot(a.astype(BF16), b.astype(BF16), preferred_element_type=F32)


def _mm_nt(a, b):
    return lax.dot_general(a.astype(BF16), b.astype(BF16), (((1,), (1,)), ((), ())),
                           preferred_element_type=F32)


def _mm_exact_rhs(a, b_bf16):
    hi = a.astype(BF16)
    lo = (a - hi.astype(F32)).astype(BF16)
    d = lambda p: jnp.dot(p, b_bf16, preferred_element_type=F32)
    return d(hi) + d(lo)


def _rms(x, g):
    return x * lax.rsqrt(jnp.mean(x * x, axis=-1, keepdims=True) + RMS_EPS) * g


def _sigmoid(x):
    return 1.0 / (1.0 + jnp.exp(-x))


def _full(shape):
    n = len(shape)
    return pl.BlockSpec(shape, lambda *_: (0,) * n)


def _resident(shape):
    n = len(shape)
    return pl.BlockSpec(shape, lambda *_: (0,) * n, pipeline_mode=pl.Buffered(1))


def _ffn_kernel(x_ref, g_ref, win_ref, wout_ref, o_ref, act_ref):
    x = x_ref[...]
    hn = _rms(x, g_ref[...]).astype(BF16)
    d_ff = wout_ref.shape[0]
    for c in range(d_ff // FFN_COL_TILE):
        lo = c * FFN_COL_TILE
        gate = jnp.dot(hn, win_ref[:, lo:lo + FFN_COL_TILE], preferred_element_type=F32)
        up = jnp.dot(hn, win_ref[:, d_ff + lo:d_ff + lo + FFN_COL_TILE],
                     preferred_element_type=F32)
        act_ref[:, lo:lo + FFN_COL_TILE] = (gate * _sigmoid(gate) * up).astype(BF16)
    o_ref[...] = x + 0.5 * jnp.dot(act_ref[...], wout_ref[...], preferred_element_type=F32)


def _ffn(x, norms, w_in, w_out, idx):
    t, d = x.shape
    d_ff = w_out.shape[1]
    tm = TOKEN_TILE
    return pl.pallas_call(
        _ffn_kernel,
        grid=(t // tm,),
        in_specs=[
            pl.BlockSpec((tm, d), lambda i: (i, 0)),
            pl.BlockSpec((None, 1, d), lambda i: (idx, 0, 0)),
            pl.BlockSpec((None, d, 2 * d_ff), lambda i: (idx, 0, 0), pipeline_mode=pl.Buffered(1)),
            pl.BlockSpec((None, d_ff, d), lambda i: (idx, 0, 0), pipeline_mode=pl.Buffered(1)),
        ],
        out_specs=pl.BlockSpec((tm, d), lambda i: (i, 0)),
        out_shape=jax.ShapeDtypeStruct((t, d), F32),
        scratch_shapes=[pltpu.VMEM((tm, d_ff), BF16)],
        compiler_params=_params("parallel"),
        name="ffn",
    )(x, norms, w_in, w_out)


def _head_sum(x, e_red):
    return jnp.dot(x.astype(BF16), e_red, preferred_element_type=F32)


def _spread(c, e_exp):
    return _mm_exact_rhs(c, e_exp)


def _head_rms(y, e_red, e_exp, gain):
    ms = _head_sum(y * y, e_red) * (1.0 / HEAD_DIM)
    return y * _spread(lax.rsqrt(ms + RMS_EPS), e_exp) * gain


def _rwkv_proj_kernel(*refs, has_vres):
    if has_vres:
        (x_ref, xp_ref, gn_ref, mu_ref, wrkv_ref, w0_ref, w1_ref, w2_ref, a0_ref, a1_ref,
         a2_ref, g1_ref, g2_ref, kk_ref, ka_ref, ered_ref, eexp_ref, v0_ref, v1_ref, v2_ref,
         vf_ref, r_o, ld_o, k_o, v_o, na_o, nb_o, g_o) = refs
    else:
        (x_ref, xp_ref, gn_ref, mu_ref, wrkv_ref, w0_ref, w1_ref, w2_ref, a0_ref, a1_ref,
         a2_ref, g1_ref, g2_ref, kk_ref, ka_ref, ered_ref, eexp_ref,
         r_o, ld_o, k_o, v_o, na_o, nb_o, g_o) = refs
    i = pl.program_id(0)
    gn = gn_ref[...]
    h = _rms(x_ref[...], gn)
    hp = _rms(xp_ref[7:8, :], gn) * jnp.where(i > 0, 1.0, 0.0)
    rows = lax.broadcasted_iota(jnp.int32, h.shape, 0)
    hs = jnp.where(rows == 0, hp, pltpu.roll(h, 1, axis=0))
    xx = hs - h
    mu = mu_ref[...]
    xr, xw, xk, xv, xa, xg = [h + xx * mu[j:j + 1] for j in range(6)]
    r = _mm(xr, wrkv_ref[0])
    k = _mm(xk, wrkv_ref[1])
    v = _mm(xv, wrkv_ref[2])
    w = w0_ref[...] + _mm(jnp.tanh(_mm(xw, w1_ref[...])), w2_ref[...])
    z = -w
    wl = -(jnp.maximum(z, 0.0) + jnp.log(1.0 + jnp.exp(-jnp.abs(z)))) - 0.5
    ld_o[...] = -jnp.exp(wl)
    a = _sigmoid(a0_ref[...] + _mm(_mm(xa, a1_ref[...]), a2_ref[...]))
    g_o[...] = _mm(_sigmoid(_mm(xg, g1_ref[...])), g2_ref[...])
    if has_vres:
        v = v + (vf_ref[...] - v) * _sigmoid(v0_ref[...] + _mm(_mm(xv, v1_ref[...]), v2_ref[...]))
    kkr = k * kk_ref[...]
    norm = jnp.sqrt(_head_sum(kkr * kkr, ered_ref[...]))
    kk = kkr * _spread(1.0 / jnp.maximum(norm, 1e-12), eexp_ref[...])
    r_o[...] = r
    k_o[...] = k * (1.0 + (a - 1.0) * ka_ref[...])
    v_o[...] = v
    na_o[...] = -kk
    nb_o[...] = kk * a


def _rwkv_proj(x, gn, mu, wrkv, w0, w1, w2, a0, a1, a2, g1, g2, k_k, k_a, e_red, e_exp, vres):
    t, d = x.shape
    tm = TOKEN_TILE // 2
    row = pl.BlockSpec((tm, d), lambda i: (i, 0))
    prev = pl.BlockSpec((8, d), lambda i: (jnp.maximum(i * (tm // 8) - 1, 0), 0))
    args = [x, x, gn, mu, wrkv, w0, w1, w2, a0, a1, a2, g1, g2, k_k, k_a, e_red, e_exp]
    specs = [row, prev, _full(gn.shape), _full(mu.shape), _resident(wrkv.shape), _full(w0.shape),
             _full(w1.shape), _full(w2.shape), _full(a0.shape), _full(a1.shape), _full(a2.shape),
             _full(g1.shape), _full(g2.shape), _full(k_k.shape), _full(k_a.shape),
             _full(e_red.shape), _full(e_exp.shape)]
    if vres is not None:
        v0, v1, v2, v_first = vres
        args += [v0, v1, v2, v_first]
        specs += [_full(v0.shape), _full(v1.shape), _full(v2.shape), row]
    return pl.pallas_call(
        functools.partial(_rwkv_proj_kernel, has_vres=vres is not None),
        grid=(t // tm,),
        in_specs=specs,
        out_specs=[row] * 7,
        out_shape=[jax.ShapeDtypeStruct((t, d), F32)] * 7,
        compiler_params=_params("parallel"),
        name="rwkv_proj",
    )(*args)


def _interleave(primary, filler):
    out = []
    for i in range(max(len(primary), len(filler))):
        if i < len(filler):
            out.append(filler[i])
        if i < len(primary):
            out.append(primary[i])
    return out


def _rwkv_scan_kernel(r_ref, ld_ref, k_ref, v_ref, na_ref, nb_ref, y_ref, s_ref):
    n_pairs = r_ref.shape[1] // LANES
    n_chunks = r_ref.shape[0] // CHUNK
    L = CHUNK
    assert 2 * L == LANES and L == HEAD_DIM and n_chunks % 2 == 0

    @pl.when(pl.program_id(0) == 0)
    def _():
        s_ref[...] = jnp.zeros_like(s_ref)

    ri = lax.broadcasted_iota(jnp.int32, (2 * L, 2 * L), 0)
    ci = lax.broadcasted_iota(jnp.int32, (2 * L, 2 * L), 1)
    same_head = (ri < L) == (ci < L)
    other_head = jnp.logical_not(same_head)
    strict = (ci & (L - 1)) < (ri & (L - 1))
    incl = (ci & (L - 1)) <= (ri & (L - 1))
    eye = (ri == ci).astype(F32)
    head0 = lax.broadcasted_iota(jnp.int32, (L, LANES), 1) < HEAD_DIM
    first_head = lax.broadcasted_iota(jnp.int32, (2 * L, LANES), 1) < HEAD_DIM
    tri = (lax.broadcasted_iota(jnp.int32, (L, L), 0)
           >= lax.broadcasted_iota(jnp.int32, (L, L), 1)).astype(BF16)
    zero = jnp.zeros((2 * L, LANES), BF16)
    pairs = range(n_pairs)
    lanes_of = lambda a, p: a[:, p * LANES:(p + 1) * LANES]
    stack2 = lambda a, b: jnp.concatenate([a, b], axis=0)

    state = [s_ref[p] for p in pairs]

    def prep_stages(chunks):
        items = [(c, p) for c in chunks for p in pairs]
        d = {}

        def elementwise():
            for c in chunks:
                rows = slice(c * L, (c + 1) * L)
                r, ld, k = r_ref[rows, :], ld_ref[rows, :], k_ref[rows, :]
                v, na, nb = v_ref[rows, :], na_ref[rows, :], nb_ref[rows, :]
                l1 = ld.astype(BF16)
                rem = ld - l1.astype(F32)
                l2 = rem.astype(BF16)
                l3 = (rem - l2.astype(F32)).astype(BF16)
                tri_dot = lambda a: jnp.dot(tri, a, preferred_element_type=F32)
                cum = tri_dot(l1) + (tri_dot(l2) + tri_dot(l3))
                cum_last = cum[L - 1:L, :]
                w_inv = jnp.exp(-cum)
                w_rel = jnp.exp(cum_last - cum)
                d["w_last", c] = jnp.exp(cum_last)
                d["x", c] = stack2(r * jnp.exp(cum), na * jnp.exp(cum - ld)).astype(BF16)
                d["bk", c] = stack2(nb * w_inv, k * w_inv).astype(BF16)
                d["kb", c] = stack2(k * w_inv, nb * w_inv).astype(BF16)
                d["keys", c] = stack2(nb * w_rel, k * w_rel).astype(BF16)
                d["vv", c] = stack2(v, v).astype(BF16)
                d["v", c] = v

        def gram():
            for c, p in items:
                rhs = stack2(jnp.where(first_head, lanes_of(d["bk", c], p), zero),
                             jnp.where(first_head, zero, lanes_of(d["kb", c], p)))
                d["g", c, p] = _mm_nt(lanes_of(d["x", c], p), rhs)

        def split():
            for c, p in items:
                g = d.pop(("g", c, p))
                g0, g1 = g[:, :LANES], g[:, LANES:]
                qs = stack2(g0[L:], g1[L:])
                ms = stack2(g0[:L], g1[:L])
                nil = jnp.where(same_head & strict, qs, 0.0)
                d["m_uv", c, p] = jnp.concatenate(
                    [jnp.where(same_head & incl, ms, 0.0),
                     jnp.where(other_head & incl, ms, 0.0)], axis=1).astype(BF16)
                d["akv", c, p] = _mm(jnp.where(other_head & strict, qs, 0.0),
                                     lanes_of(d["vv", c], p))
                d["t", c, p] = eye + nil
                d["pw", c, p] = _mm(nil, nil)

        def double():
            for c, p in items:
                pw = d["pw", c, p]
                t = d["t", c, p]
                both = _mm(pw, jnp.concatenate([t, pw], axis=1))
                d["t", c, p] = t + both[:, :LANES]
                d["pw", c, p] = both[:, LANES:]

        def finish():
            for c, p in items:
                t = d["t", c, p]
                d["t", c, p] = (t + _mm(d.pop(("pw", c, p)), t)).astype(BF16)

        return [elementwise, gram, split, double, double, double, double, finish], d

    def chunk_stages(c, d):
        loc = {}

        def project():
            for p in pairs:
                loc["xs", p] = _mm_nt(lanes_of(d["x", c], p), state[p])

        def solve():
            for p in pairs:
                xs = loc["xs", p]
                u = _mm(d["t", c, p], stack2(xs[L:], xs[L:]) + d["akv", c, p])
                loc["u", p] = jnp.where(head0, u[:L], u[L:])

        def emit():
            for p in pairs:
                xs, u = loc["xs", p], loc["u", p]
                ub = u.astype(BF16)
                vvp = lanes_of(d["vv", c], p)
                y = stack2(xs[:L], xs[:L]) + _mm(d["m_uv", c, p],
                                                 jnp.concatenate([ub, ub, vvp], axis=0))
                y_ref[c * L:(c + 1) * L, p * LANES:(p + 1) * LANES] = jnp.where(
                    head0, y[:L], y[L:])
                vals = stack2(u, lanes_of(d["v", c], p))
                upd = _mm(vals.T, lanes_of(d["keys", c], p))
                state[p] = jnp.where(
                    same_head, state[p] * lanes_of(d["w_last", c], p) + upd, 0.0)

        return [project, solve, emit]

    groups = [(c, c + 1) for c in range(0, n_chunks, 2)]
    prep, data = prep_stages(groups[0])
    for stage in prep:
        stage()
    for gi, grp in enumerate(groups):
        chain = chunk_stages(grp[0], data) + chunk_stages(grp[1], data)
        if gi + 1 < len(groups):
            prep, nxt = prep_stages(groups[gi + 1])
        else:
            prep, nxt = [], None
        for stage in _interleave(chain, prep):
            stage()
        data = nxt
    for p in pairs:
        s_ref[p] = state[p]


def _rwkv_scan(r, ld, k, v, na, nb):
    t, d = r.shape
    blk = pl.BlockSpec((SCAN_BLOCK, d), lambda c: (c, 0))
    return pl.pallas_call(
        _rwkv_scan_kernel,
        grid=(t // SCAN_BLOCK,),
        in_specs=[blk] * 6,
        out_specs=blk,
        out_shape=jax.ShapeDtypeStruct((t, d), F32),
        scratch_shapes=[pltpu.VMEM((d // LANES, LANES, LANES), F32)],
        compiler_params=_params("arbitrary"),
        name="rwkv_scan",
    )(r, ld, k, v, na, nb)


def _rwkv_post_kernel(y_ref, r_ref, k_ref, v_ref, g_ref, x_ref, lnw_ref, lnb_ref, rk_ref,
                      wo_ref, ered_ref, eexp_ref, o_ref):
    e_red = ered_ref[...]
    e_exp = eexp_ref[...]
    y = y_ref[...]
    inv_n = 1.0 / HEAD_DIM
    yc = y - _spread(_head_sum(y, e_red) * inv_n, e_exp)
    var = _head_sum(yc * yc, e_red) * inv_n
    yn = yc * _spread(lax.rsqrt(var + GN_EPS), e_exp) * lnw_ref[...] + lnb_ref[...]
    bonus = _spread(_head_sum(r_ref[...] * k_ref[...] * rk_ref[...], e_red), e_exp) * v_ref[...]
    z = (yn + bonus) * g_ref[...]
    o_ref[...] = x_ref[...] + _mm(z, wo_ref[...])


def _rwkv_post(y, r, k, v, g, x, ln_w, ln_b, r_k, w_o, e_red, e_exp):
    t, d = x.shape
    tm = TOKEN_TILE
    row = pl.BlockSpec((tm, d), lambda i: (i, 0))
    return pl.pallas_call(
        _rwkv_post_kernel,
        grid=(t // tm,),
        in_specs=[row] * 6 + [_full(ln_w.shape), _full(ln_b.shape), _full(r_k.shape),
                              _resident(w_o.shape), _full(e_red.shape), _full(e_exp.shape)],
        out_specs=row,
        out_shape=jax.ShapeDtypeStruct((t, d), F32),
        compiler_params=_params("parallel"),
        name="rwkv_post",
    )(y, r, k, v, g, x, ln_w, ln_b, r_k, w_o, e_red, e_exp)


def _store_dilated(dst_ref, y, stage_ref, dil):
    rows, d = y.shape
    if dil == 1:
        dst_ref[...] = y.astype(dst_ref.dtype)
        return
    n_tiles = d // LANES
    for c in range(n_tiles):
        stage_ref[c] = y[:, c * LANES:(c + 1) * LANES]
    for r in range(dil):
        for c in range(n_tiles):
            lo = r * d + c * LANES
            dst_ref[:, lo:lo + LANES] = (
                stage_ref[c, pl.ds(r, rows // dil, stride=dil), :].astype(dst_ref.dtype))


def _kv_proj_kernel(x_ref, gn_ref, w_ref, gain_ref, ered_ref, eexp_ref, o0_ref, o1_ref, o2_ref,
                    hn_ref, stage_ref):
    j = pl.program_id(1)
    n_groups = len(DIL_GROUPS)

    @pl.when(j == 0)
    def _():
        hn_ref[...] = _rms(x_ref[...], gn_ref[...]).astype(BF16)

    y = jnp.dot(hn_ref[...], w_ref[...], preferred_element_type=F32)
    outs = (o0_ref, o1_ref, o2_ref)
    for jj in range(2 * n_groups):
        is_value, grp = divmod(jj, n_groups)

        @pl.when(j == jj)
        def _(is_value=is_value, grp=grp):
            val = y if is_value else _head_rms(y, ered_ref[...], eexp_ref[...], gain_ref[...])
            _store_dilated(outs[grp].at[is_value], val, stage_ref, DIL_GROUPS[grp][1])


def _kv_proj(x, gn, w_kv, gains, e_red, e_exp):
    t, d = x.shape
    tm = TOKEN_TILE
    n_blk = w_kv.shape[1] // d
    n_groups = len(DIL_GROUPS)
    return pl.pallas_call(
        _kv_proj_kernel,
        grid=(t // tm, n_blk),
        in_specs=[
            pl.BlockSpec((tm, d), lambda i, j: (i, 0)),
            _full(gn.shape),
            pl.BlockSpec((d, d), lambda i, j: (0, j)),
            pl.BlockSpec((None, 1, d), lambda i, j: (jnp.minimum(j, n_groups - 1), 0, 0)),
            _full(e_red.shape), _full(e_exp.shape),
        ],
        out_specs=[pl.BlockSpec((2, tm // dil, dil * d), lambda i, j: (0, i, 0))
                   for _, dil in DIL_GROUPS],
        out_shape=[jax.ShapeDtypeStruct((2, t // dil, dil * d), BF16) for _, dil in DIL_GROUPS],
        scratch_shapes=[pltpu.VMEM((tm, d), BF16), pltpu.VMEM((d // LANES, tm, LANES), F32)],
        compiler_params=_params("parallel", "arbitrary"),
        name="kv_proj",
    )(x, gn, w_kv, gains, e_red, e_exp)


def _q_proj_kernel(x_ref, gn_ref, w_ref, gain_ref, ered_ref, eexp_ref, o0_ref, o1_ref, o2_ref,
                   hn_ref, stage_ref):
    j = pl.program_id(1)

    @pl.when(j == 0)
    def _():
        hn_ref[...] = _rms(x_ref[...], gn_ref[...]).astype(BF16)

    y = jnp.dot(hn_ref[...], w_ref[...], preferred_element_type=F32)
    y = _head_rms(y, ered_ref[...], eexp_ref[...], gain_ref[...]) * HEAD_DIM ** -0.5
    for grp, out_ref in enumerate((o0_ref, o1_ref, o2_ref)):

        @pl.when(j == grp)
        def _(grp=grp, out_ref=out_ref):
            _store_dilated(out_ref, y, stage_ref, DIL_GROUPS[grp][1])


def _q_proj(x, gn, w_q, gains, e_red, e_exp):
    t, d = x.shape
    tm = TOKEN_TILE
    n_blk = w_q.shape[1] // d
    return pl.pallas_call(
        _q_proj_kernel,
        grid=(t // tm, n_blk),
        in_specs=[
            pl.BlockSpec((tm, d), lambda i, j: (i, 0)),
            _full(gn.shape),
            pl.BlockSpec((d, d), lambda i, j: (0, j)),
            pl.BlockSpec((None, 1, d), lambda i, j: (j, 0, 0)),
            _full(e_red.shape), _full(e_exp.shape),
        ],
        out_specs=[pl.BlockSpec((tm // dil, dil * d), lambda i, j: (i, 0)) for _, dil in DIL_GROUPS],
        out_shape=[jax.ShapeDtypeStruct((t // dil, dil * d), BF16) for _, dil in DIL_GROUPS],
        scratch_shapes=[pltpu.VMEM((tm, d), BF16), pltpu.VMEM((d // LANES, tm, LANES), F32)],
        compiler_params=_params("parallel", "arbitrary"),
        name="q_proj",
    )(x, gn, w_q, gains, e_red, e_exp)


def _dil_attn_kernel(q_ref, kp_ref, kc_ref, vp_ref, vc_ref, o_ref, m_ref, l_ref):
    n = pl.program_id(1)
    blk = ATT_BLK
    n_pairs = q_ref.shape[1] // LANES
    n_sub = q_ref.shape[0] // blk
    qi = lax.broadcasted_iota(jnp.int32, (blk, 2 * blk), 0)
    kj = lax.broadcasted_iota(jnp.int32, (blk, 2 * blk), 1)
    band = (kj >= qi) & (kj <= qi + blk)
    valid = [band & ((n > 0) | (kj >= blk))] + [band] * (n_sub - 1)
    lane = lax.broadcasted_iota(jnp.int32, (blk, LANES), 1)
    head0 = lane < HEAD_DIM
    dn_nt = (((1,), (1,)), ((), ()))
    zero = jnp.zeros((blk, LANES), BF16)
    lanes_of = lambda a, p: a[:, p * LANES:(p + 1) * LANES]

    def keys_of(prev_ref, cur_ref, s, p):
        if s == 0:
            return jnp.concatenate([lanes_of(prev_ref, p), lanes_of(cur_ref, p)[:blk]], axis=0)
        return lanes_of(cur_ref, p)[(s - 1) * blk:(s + 1) * blk]

    items = [(s, p, hh) for s in range(n_sub) for p in range(n_pairs) for hh in range(2)]
    scores = {}
    for s, p, hh in items:
        qp = lanes_of(q_ref, p)[s * blk:(s + 1) * blk]
        qm = jnp.where(head0, qp, zero) if hh == 0 else jnp.where(head0, zero, qp)
        scores[s, p, hh] = lax.dot_general(qm, keys_of(kp_ref, kc_ref, s, p), dn_nt,
                                           preferred_element_type=F32)
    probs = {}
    m_acc = [jnp.zeros((blk, LANES), F32) for _ in range(n_sub)]
    l_acc = [jnp.ones((blk, LANES), F32) for _ in range(n_sub)]
    for s, p, hh in items:
        sc = jnp.where(valid[s], scores.pop((s, p, hh)), NEG_INF)
        mx = jnp.max(sc, axis=-1, keepdims=True)
        pe = jnp.exp(sc - mx)
        hit = lane == 2 * p + hh
        m_acc[s] = jnp.where(hit, mx, m_acc[s])
        l_acc[s] = jnp.where(hit, jnp.sum(pe, axis=-1, keepdims=True), l_acc[s])
        probs[s, p, hh] = pe.astype(BF16)
    for s in range(n_sub):
        for p in range(n_pairs):
            vcat = keys_of(vp_ref, vc_ref, s, p)
            o0 = jnp.dot(probs.pop((s, p, 0)), vcat, preferred_element_type=F32)
            o1 = jnp.dot(probs.pop((s, p, 1)), vcat, preferred_element_type=F32)
            o_ref[s * blk:(s + 1) * blk, p * LANES:(p + 1) * LANES] = jnp.where(head0, o0, o1)
        m_ref[s * blk:(s + 1) * blk, :] = m_acc[s]
        l_ref[s * blk:(s + 1) * blk, :] = l_acc[s]


def _dil_attn(q, kv, dil):
    m_rows, wide = q.shape
    d = wide // dil
    blk = ATT_BLK
    step = ATT_SUB * blk
    cur = lambda a: pl.BlockSpec((None, step, d), lambda r, n: (a, n, r))
    prev = lambda a: pl.BlockSpec((None, blk, d),
                                  lambda r, n: (a, jnp.maximum(ATT_SUB * n - 1, 0), r))
    return pl.pallas_call(
        _dil_attn_kernel,
        grid=(dil, m_rows // step),
        in_specs=[pl.BlockSpec((step, d), lambda r, n: (n, r)), prev(0), cur(0), prev(1), cur(1)],
        out_specs=[pl.BlockSpec((step, d), lambda r, n: (n, r)),
                   pl.BlockSpec((step, LANES), lambda r, n: (n, r)),
                   pl.BlockSpec((step, LANES), lambda r, n: (n, r))],
        out_shape=[jax.ShapeDtypeStruct((m_rows, dil * d), F32),
                   jax.ShapeDtypeStruct((m_rows, dil * LANES), F32),
                   jax.ShapeDtypeStruct((m_rows, dil * LANES), F32)],
        compiler_params=_params("parallel", "parallel"),
        name="dil_attn",
    )(q, kv, kv, kv, kv)


def _load_dilated(src_ref, stage_ref, dil):
    if dil == 1:
        return src_ref[...]
    sub = src_ref.shape[0]
    n_tiles = src_ref.shape[1] // dil // LANES
    for r in range(dil):
        for c in range(n_tiles):
            lo = (r * n_tiles + c) * LANES
            stage_ref[c, pl.ds(r, sub, stride=dil), :] = src_ref[:, lo:lo + LANES]
    return jnp.concatenate([stage_ref[c] for c in range(n_tiles)], axis=1)


def _attn_combine_kernel(o0, o1, o2, m0, m1, m2, l0, l1, l2, x_ref, wo_ref, eexp_ref, out_ref,
                         os1, os2, ms1, ms2, ls1, ls2):
    dils = [dil for _, dil in DIL_GROUPS]
    ms = [_load_dilated(ref, st, dil) for ref, st, dil in zip((m0, m1, m2), (None, ms1, ms2), dils)]
    ls = [_load_dilated(ref, st, dil) for ref, st, dil in zip((l0, l1, l2), (None, ls1, ls2), dils)]
    top = jnp.maximum(jnp.maximum(ms[0], ms[1]), ms[2])
    es = [jnp.exp(mg - top) for mg in ms]
    den = es[0] * ls[0] + es[1] * ls[1] + es[2] * ls[2]
    e_exp = eexp_ref[...]
    acc = None
    for eg, ref, st, dil in zip(es, (o0, o1, o2), (None, os1, os2), dils):
        term = _spread(eg / den, e_exp) * _load_dilated(ref, st, dil)
        acc = term if acc is None else acc + term
    out_ref[...] = x_ref[...] + _mm(acc, wo_ref[...])


def _attn_combine(parts, x, w_o, e_exp):
    t, d = x.shape
    tm = TOKEN_TILE
    row = pl.BlockSpec((tm, d), lambda i: (i, 0))
    dils = [dil for _, dil in DIL_GROUPS]
    o_specs = [pl.BlockSpec((tm // dil, dil * d), lambda i: (i, 0)) for dil in dils]
    s_specs = [pl.BlockSpec((tm // dil, dil * LANES), lambda i: (i, 0)) for dil in dils]
    os_, ms_, ls_ = zip(*parts)
    return pl.pallas_call(
        _attn_combine_kernel,
        grid=(t // tm,),
        in_specs=o_specs + s_specs + s_specs + [row, _resident(w_o.shape), _full(e_exp.shape)],
        out_specs=row,
        out_shape=jax.ShapeDtypeStruct((t, d), F32),
        scratch_shapes=([pltpu.VMEM((d // LANES, tm, LANES), F32)] * 2
                        + [pltpu.VMEM((1, tm, LANES), F32)] * 4),
        compiler_params=_params("parallel"),
        name="attn_combine",
    )(*os_, *ms_, *ls_, x, w_o, e_exp)


def kernel(x, ffn_norm, ffn_w_in, ffn_w_out, mix_norm, rwkv_mu, rwkv_w_rkv, rwkv_w0, rwkv_w1, rwkv_w2, rwkv_a0, rwkv_a1, rwkv_a2, rwkv_v0, rwkv_v1, rwkv_v2, rwkv_g1, rwkv_g2, rwkv_k_k, rwkv_k_a, rwkv_r_k, rwkv_ln_w, rwkv_ln_b, rwkv_w_o, kv_norm, w_kv, k_norm, attn_w_q, q_norm, attn_w_o):
    bsz, t, d = x.shape
    assert bsz == 1 and d % LANES == 0 and t % max(w for w, _ in DIL_GROUPS) == 0
    assert all(w // dil == ATT_BLK for w, dil in DIL_GROUPS)
    depth = ffn_norm.shape[0]
    n_a = rwkv_mu.shape[0]
    n_groups = len(DIL_GROUPS)
    row = lambda p: p.reshape(1, -1)

    ffn_g = ffn_norm.reshape(2 * depth, 1, d)
    ffn_wi = ffn_w_in.astype(BF16).reshape(2 * depth, d, -1)
    ffn_wo = ffn_w_out.astype(BF16).reshape(2 * depth, -1, d)
    head_of_lane = jnp.arange(d) // HEAD_DIM
    e_exp = (jnp.arange(LANES)[:, None] == head_of_lane[None, :]).astype(BF16)
    e_red = e_exp.T

    xs = x.reshape(t, d)
    v_first = None
    kvs = None
    for l in range(depth):
        xs = _ffn(xs, ffn_g, ffn_wi, ffn_wo, 2 * l)
        if l < n_a:
            vres = None
            if l > 0:
                vres = (row(rwkv_v0[l - 1]), rwkv_v1[l - 1].astype(BF16),
                        rwkv_v2[l - 1].astype(BF16), v_first)
            r, ld, k, v, na, nb, g = _rwkv_proj(
                xs, row(mix_norm[l]), rwkv_mu[l], rwkv_w_rkv[l].astype(BF16), row(rwkv_w0[l]),
                rwkv_w1[l].astype(BF16), rwkv_w2[l].astype(BF16), row(rwkv_a0[l]),
                rwkv_a1[l].astype(BF16), rwkv_a2[l].astype(BF16), rwkv_g1[l].astype(BF16),
                rwkv_g2[l].astype(BF16), row(rwkv_k_k[l]), row(rwkv_k_a[l]), e_red, e_exp, vres)
            if l == 0:
                v_first = v
            y = _rwkv_scan(r, ld, k, v, na, nb)
            xs = _rwkv_post(y, r, k, v, g, xs, row(rwkv_ln_w[l]), row(rwkv_ln_b[l]),
                            row(rwkv_r_k[l]), rwkv_w_o[l].astype(BF16), e_red, e_exp)
        else:
            i = l - n_a
            q_gain = jnp.tile(q_norm[i], (1, d // HEAD_DIM)).reshape(n_groups, 1, d)
            qs = _q_proj(xs, row(mix_norm[l]), attn_w_q[i].astype(BF16), q_gain, e_red, e_exp)
            parts = [_dil_attn(qs[grp], kvs[grp], DIL_GROUPS[grp][1]) for grp in range(n_groups)]
            xs = _attn_combine(parts, xs, attn_w_o[i].astype(BF16), e_exp)
        xs = _ffn(xs, ffn_g, ffn_wi, ffn_wo, 2 * l + 1)
        if l == n_a - 1:
            k_gain = jnp.tile(k_norm, (1, d // HEAD_DIM)).reshape(n_groups, 1, d)
            kvs = _kv_proj(xs, row(kv_norm), w_kv.astype(BF16), k_gain, e_red, e_exp)
    return xs.reshape(bsz, t, d)
```

```python
import functools

import jax
import jax.numpy as jnp
from jax import lax
from jax.experimental import pallas as pl
from jax.experimental.pallas import tpu as pltpu

F32 = jnp.float32
BF16 = jnp.bfloat16

RMS_EPS = 1e-6
GN_EPS = 64e-5
HEAD_DIM = 64
LANES = 128
DIL_GROUPS = ((128, 1), (512, 4), (2048, 16))
ATT_BLK = 128
ATT_SUB = 2
NEG_INF = -1e30
LOG2_E = 1.4426950408889634
CHUNK = 64
SCAN_BLOCK = 256
TOKEN_TILE = 512
FFN_COL_TILE = 256
VMEM_LIMIT = 56 * 1024 * 1024


def _params(*sem):
    return pltpu.CompilerParams(dimension_semantics=sem, vmem_limit_bytes=VMEM_LIMIT)


def _mm(a, b):
    return jnp.dot(a.astype(BF16), b.astype(BF16), preferred_element_type=F32)


def _mm_nt(a, b):
    return lax.dot_general(a.astype(BF16), b.astype(BF16), (((1,), (1,)), ((), ())),
                           preferred_element_type=F32)


def _mm_exact_rhs(a, b_bf16):
    hi = a.astype(BF16)
    lo = (a - hi.astype(F32)).astype(BF16)
    d = lambda p: jnp.dot(p, b_bf16, preferred_element_type=F32)
    return d(hi) + d(lo)


def _rms(x, g):
    return x * lax.rsqrt(jnp.mean(x * x, axis=-1, keepdims=True) + RMS_EPS) * g


def _sigmoid(x):
    return 1.0 / (1.0 + jnp.exp(-x))


def _full(shape):
    n = len(shape)
    return pl.BlockSpec(shape, lambda *_: (0,) * n)


def _resident(shape):
    n = len(shape)
    return pl.BlockSpec(shape, lambda *_: (0,) * n, pipeline_mode=pl.Buffered(1))


def _ffn_kernel(x_ref, g_ref, win_ref, wout_ref, o_ref, act_ref):
    x = x_ref[...]
    hn = _rms(x, g_ref[...]).astype(BF16)
    d_ff = wout_ref.shape[0]
    for c in range(d_ff // FFN_COL_TILE):
        lo = c * FFN_COL_TILE
        gate = jnp.dot(hn, win_ref[:, lo:lo + FFN_COL_TILE], preferred_element_type=F32)
        up = jnp.dot(hn, win_ref[:, d_ff + lo:d_ff + lo + FFN_COL_TILE],
                     preferred_element_type=F32)
        act_ref[:, lo:lo + FFN_COL_TILE] = (gate * _sigmoid(gate) * up).astype(BF16)
    o_ref[...] = x + 0.5 * jnp.dot(act_ref[...], wout_ref[...], preferred_element_type=F32)


def _ffn(x, norms, w_in, w_out, idx):
    t, d = x.shape
    d_ff = w_out.shape[1]
    tm = TOKEN_TILE
    return pl.pallas_call(
        _ffn_kernel,
        grid=(t // tm,),
        in_specs=[
            pl.BlockSpec((tm, d), lambda i: (i, 0)),
            pl.BlockSpec((None, 1, d), lambda i: (idx, 0, 0)),
            pl.BlockSpec((None, d, 2 * d_ff), lambda i: (idx, 0, 0), pipeline_mode=pl.Buffered(1)),
            pl.BlockSpec((None, d_ff, d), lambda i: (idx, 0, 0), pipeline_mode=pl.Buffered(1)),
        ],
        out_specs=pl.BlockSpec((tm, d), lambda i: (i, 0)),
        out_shape=jax.ShapeDtypeStruct((t, d), F32),
        scratch_shapes=[pltpu.VMEM((tm, d_ff), BF16)],
        compiler_params=_params("parallel"),
        name="ffn",
    )(x, norms, w_in, w_out)


def _head_sum(x, e_red):
    return jnp.dot(x.astype(BF16), e_red, preferred_element_type=F32)


def _spread(c, e_exp):
    return _mm_exact_rhs(c, e_exp)


def _rwkv_proj_kernel(*refs, has_vres):
    if has_vres:
        (x_ref, xp_ref, gn_ref, mu_ref, wrkv_ref, w0_ref, w1_ref, w2_ref, a0_ref, a1_ref,
         a2_ref, g1_ref, g2_ref, kk_ref, ka_ref, ered_ref, eexp_ref, v0_ref, v1_ref, v2_ref,
         vf_ref, r_o, ld_o, k_o, v_o, na_o, nb_o, g_o) = refs
    else:
        (x_ref, xp_ref, gn_ref, mu_ref, wrkv_ref, w0_ref, w1_ref, w2_ref, a0_ref, a1_ref,
         a2_ref, g1_ref, g2_ref, kk_ref, ka_ref, ered_ref, eexp_ref,
         r_o, ld_o, k_o, v_o, na_o, nb_o, g_o) = refs
    i = pl.program_id(0)
    gn = gn_ref[...]
    h = _rms(x_ref[...], gn)
    hp = _rms(xp_ref[7:8, :], gn) * jnp.where(i > 0, 1.0, 0.0)
    rows = lax.broadcasted_iota(jnp.int32, h.shape, 0)
    hs = jnp.where(rows == 0, hp, pltpu.roll(h, 1, axis=0))
    xx = hs - h
    mu = mu_ref[...]
    xr, xw, xk, xv, xa, xg = [h + xx * mu[j:j + 1] for j in range(6)]
    w_dn = _mm(xw, w1_ref[...])
    a_dn = _mm(xa, a1_ref[...])
    g_dn = _mm(xg, g1_ref[...])
    v_dn = _mm(xv, v1_ref[...]) if has_vres else None
    r = _mm(xr, wrkv_ref[0])
    k = _mm(xk, wrkv_ref[1])
    v = _mm(xv, wrkv_ref[2])
    w = w0_ref[...] + _mm(jnp.tanh(w_dn), w2_ref[...])
    z = -w
    wl = -(jnp.maximum(z, 0.0) + jnp.log(1.0 + jnp.exp(-jnp.abs(z)))) - 0.5
    ld_o[...] = -jnp.exp(wl)
    a = _sigmoid(a0_ref[...] + _mm(a_dn, a2_ref[...]))
    g_o[...] = _mm(_sigmoid(g_dn), g2_ref[...])
    if has_vres:
        v = v + (vf_ref[...] - v) * _sigmoid(v0_ref[...] + _mm(v_dn, v2_ref[...]))
    kkr = k * kk_ref[...]
    norm = jnp.sqrt(_head_sum(kkr * kkr, ered_ref[...]))
    kk = kkr * _spread(1.0 / jnp.maximum(norm, 1e-12), eexp_ref[...])
    r_o[...] = r
    k_o[...] = k * (1.0 + (a - 1.0) * ka_ref[...])
    v_o[...] = v
    na_o[...] = -kk
    nb_o[...] = kk * a


def _rwkv_proj(x, gn, mu, wrkv, w0, w1, w2, a0, a1, a2, g1, g2, k_k, k_a, e_red, e_exp, vres):
    t, d = x.shape
    tm = TOKEN_TILE // 2
    row = pl.BlockSpec((tm, d), lambda i: (i, 0))
    prev = pl.BlockSpec((8, d), lambda i: (jnp.maximum(i * (tm // 8) - 1, 0), 0))
    args = [x, x, gn, mu, wrkv, w0, w1, w2, a0, a1, a2, g1, g2, k_k, k_a, e_red, e_exp]
    specs = [row, prev, _full(gn.shape), _full(mu.shape), _resident(wrkv.shape), _full(w0.shape),
             _full(w1.shape), _full(w2.shape), _full(a0.shape), _full(a1.shape), _full(a2.shape),
             _full(g1.shape), _full(g2.shape), _full(k_k.shape), _full(k_a.shape),
             _full(e_red.shape), _full(e_exp.shape)]
    if vres is not None:
        v0, v1, v2, v_first = vres
        args += [v0, v1, v2, v_first]
        specs += [_full(v0.shape), _full(v1.shape), _full(v2.shape), row]
    return pl.pallas_call(
        functools.partial(_rwkv_proj_kernel, has_vres=vres is not None),
        grid=(t // tm,),
        in_specs=specs,
        out_specs=[row] * 7,
        out_shape=[jax.ShapeDtypeStruct((t, d), F32)] * 7,
        compiler_params=_params("parallel"),
        name="rwkv_proj",
    )(*args)


def _interleave(primary, filler):
    out = []
    for i in range(max(len(primary), len(filler))):
        if i < len(filler):
            out.append(filler[i])
        if i < len(primary):
            out.append(primary[i])
    return out


def _rwkv_scan_kernel(r_ref, ld_ref, k_ref, v_ref, na_ref, nb_ref, y_ref, s_ref):
    n_pairs = r_ref.shape[1] // LANES
    n_chunks = r_ref.shape[0] // CHUNK
    L = CHUNK
    assert 2 * L == LANES and L == HEAD_DIM and n_chunks % 2 == 0

    @pl.when(pl.program_id(0) == 0)
    def _():
        s_ref[...] = jnp.zeros_like(s_ref)

    ri = lax.broadcasted_iota(jnp.int32, (2 * L, 2 * L), 0)
    ci = lax.broadcasted_iota(jnp.int32, (2 * L, 2 * L), 1)
    same_head = (ri < L) == (ci < L)
    other_head = jnp.logical_not(same_head)
    strict = (ci & (L - 1)) < (ri & (L - 1))
    incl = (ci & (L - 1)) <= (ri & (L - 1))
    eye = (ri == ci).astype(F32)
    head0 = lax.broadcasted_iota(jnp.int32, (L, LANES), 1) < HEAD_DIM
    first_head = lax.broadcasted_iota(jnp.int32, (2 * L, LANES), 1) < HEAD_DIM
    tri = (lax.broadcasted_iota(jnp.int32, (L, L), 0)
           >= lax.broadcasted_iota(jnp.int32, (L, L), 1)).astype(BF16)
    zero = jnp.zeros((2 * L, LANES), BF16)
    pairs = range(n_pairs)
    lanes_of = lambda a, p: a[:, p * LANES:(p + 1) * LANES]
    stack2 = lambda a, b: jnp.concatenate([a, b], axis=0)

    state = [s_ref[p] for p in pairs]

    def prep_stages(chunks):
        items = [(c, p) for c in chunks for p in pairs]
        d = {}

        def elementwise():
            for c in chunks:
                rows = slice(c * L, (c + 1) * L)
                r, ld, k = r_ref[rows, :], ld_ref[rows, :], k_ref[rows, :]
                v, na, nb = v_ref[rows, :], na_ref[rows, :], nb_ref[rows, :]
                l1 = ld.astype(BF16)
                rem = ld - l1.astype(F32)
                l2 = rem.astype(BF16)
                l3 = (rem - l2.astype(F32)).astype(BF16)
                tri_dot = lambda a: jnp.dot(tri, a, preferred_element_type=F32)
                cum = tri_dot(l1) + (tri_dot(l2) + tri_dot(l3))
                cum_last = cum[L - 1:L, :]
                w_inv = jnp.exp(-cum)
                w_rel = jnp.exp(cum_last - cum)
                d["w_last", c] = jnp.exp(cum_last)
                d["x", c] = stack2(r * jnp.exp(cum), na * jnp.exp(cum - ld)).astype(BF16)
                d["bk", c] = stack2(nb * w_inv, k * w_inv).astype(BF16)
                d["kb", c] = stack2(k * w_inv, nb * w_inv).astype(BF16)
                d["keys", c] = stack2(nb * w_rel, k * w_rel).astype(BF16)
                d["vv", c] = stack2(v, v).astype(BF16)
                d["v", c] = v

        def gram():
            for c, p in items:
                rhs = stack2(jnp.where(first_head, lanes_of(d["bk", c], p), zero),
                             jnp.where(first_head, zero, lanes_of(d["kb", c], p)))
                d["g", c, p] = _mm_nt(lanes_of(d["x", c], p), rhs)

        def split():
            for c, p in items:
                g = d.pop(("g", c, p))
                g0, g1 = g[:, :LANES], g[:, LANES:]
                qs = stack2(g0[L:], g1[L:])
                ms = stack2(g0[:L], g1[:L])
                nil = jnp.where(same_head & strict, qs, 0.0)
                d["m_uv", c, p] = jnp.concatenate(
                    [jnp.where(same_head & incl, ms, 0.0),
                     jnp.where(other_head & incl, ms, 0.0)], axis=1).astype(BF16)
                d["akv", c, p] = _mm(jnp.where(other_head & strict, qs, 0.0),
                                     lanes_of(d["vv", c], p))
                d["t", c, p] = eye + nil
                d["pw", c, p] = _mm(nil, nil)

        def double():
            for c, p in items:
                pw = d["pw", c, p]
                t = d["t", c, p]
                both = _mm(pw, jnp.concatenate([t, pw], axis=1))
                d["t", c, p] = t + both[:, :LANES]
                d["pw", c, p] = both[:, LANES:]

        def finish():
            for c, p in items:
                t = d["t", c, p]
                d["t", c, p] = (t + _mm(d.pop(("pw", c, p)), t)).astype(BF16)

        return [elementwise, gram, split, double, double, double, double, finish], d

    def chunk_stages(c, d):
        loc = {}

        def project():
            for p in pairs:
                loc["xs", p] = _mm_nt(lanes_of(d["x", c], p), state[p])

        def solve():
            for p in pairs:
                xs = loc["xs", p]
                u = _mm(d["t", c, p], stack2(xs[L:], xs[L:]) + d["akv", c, p])
                loc["u", p] = jnp.where(head0, u[:L], u[L:])

        def emit():
            for p in pairs:
                xs, u = loc["xs", p], loc["u", p]
                ub = u.astype(BF16)
                vvp = lanes_of(d["vv", c], p)
                y = stack2(xs[:L], xs[:L]) + _mm(d["m_uv", c, p],
                                                 jnp.concatenate([ub, ub, vvp], axis=0))
                y_ref[c * L:(c + 1) * L, p * LANES:(p + 1) * LANES] = jnp.where(
                    head0, y[:L], y[L:])
                vals = stack2(u, lanes_of(d["v", c], p))
                upd = _mm(vals.T, lanes_of(d["keys", c], p))
                state[p] = jnp.where(
                    same_head, state[p] * lanes_of(d["w_last", c], p) + upd, 0.0)

        return [project, solve, emit]

    groups = [(c, c + 1) for c in range(0, n_chunks, 2)]
    prep, data = prep_stages(groups[0])
    for stage in prep:
        stage()
    for gi, grp in enumerate(groups):
        chain = chunk_stages(grp[0], data) + chunk_stages(grp[1], data)
        if gi + 1 < len(groups):
            prep, nxt = prep_stages(groups[gi + 1])
        else:
            prep, nxt = [], None
        for stage in _interleave(chain, prep):
            stage()
        data = nxt
    for p in pairs:
        s_ref[p] = state[p]


def _rwkv_scan(r, ld, k, v, na, nb):
    t, d = r.shape
    blk = pl.BlockSpec((SCAN_BLOCK, d), lambda c: (c, 0))
    return pl.pallas_call(
        _rwkv_scan_kernel,
        grid=(t // SCAN_BLOCK,),
        in_specs=[blk] * 6,
        out_specs=blk,
        out_shape=jax.ShapeDtypeStruct((t, d), F32),
        scratch_shapes=[pltpu.VMEM((d // LANES, LANES, LANES), F32)],
        compiler_params=_params("arbitrary"),
        name="rwkv_scan",
    )(r, ld, k, v, na, nb)


def _rwkv_post_kernel(y_ref, r_ref, k_ref, v_ref, g_ref, x_ref, lnw_ref, lnb_ref, rk_ref,
                      wo_ref, ered_ref, eexp_ref, o_ref):
    e_red = ered_ref[...]
    e_exp = eexp_ref[...]
    inv_n = 1.0 / HEAD_DIM
    n_sub = 4
    sub = y_ref.shape[0] // n_sub
    blocks = [slice(s * sub, (s + 1) * sub) for s in range(n_sub)]
    ys = [y_ref[b, :] for b in blocks]
    sum_y = [_head_sum(y, e_red) for y in ys]
    sum_b = [_head_sum(r_ref[b, :] * k_ref[b, :] * rk_ref[...], e_red) for b in blocks]
    yc = [y - _spread(s * inv_n, e_exp) for y, s in zip(ys, sum_y)]
    bonus = [_spread(s, e_exp) * v_ref[b, :] for s, b in zip(sum_b, blocks)]
    var = [_head_sum(c * c, e_red) * inv_n for c in yc]
    rstd = [_spread(lax.rsqrt(v + GN_EPS), e_exp) for v in var]
    z = [((c * rs * lnw_ref[...] + lnb_ref[...]) + bn) * g_ref[b, :]
         for c, rs, bn, b in zip(yc, rstd, bonus, blocks)]
    o_ref[...] = x_ref[...] + _mm(jnp.concatenate(z, axis=0), wo_ref[...])


def _rwkv_post(y, r, k, v, g, x, ln_w, ln_b, r_k, w_o, e_red, e_exp):
    t, d = x.shape
    tm = TOKEN_TILE
    row = pl.BlockSpec((tm, d), lambda i: (i, 0))
    return pl.pallas_call(
        _rwkv_post_kernel,
        grid=(t // tm,),
        in_specs=[row] * 6 + [_full(ln_w.shape), _full(ln_b.shape), _full(r_k.shape),
                              _resident(w_o.shape), _full(e_red.shape), _full(e_exp.shape)],
        out_specs=row,
        out_shape=jax.ShapeDtypeStruct((t, d), F32),
        compiler_params=_params("parallel"),
        name="rwkv_post",
    )(y, r, k, v, g, x, ln_w, ln_b, r_k, w_o, e_red, e_exp)


def _store_dilated(dst_ref, y, stage_ref, dil):
    rows, d = y.shape
    if dil == 1:
        dst_ref[...] = y.astype(dst_ref.dtype)
        return
    n_tiles = d // LANES
    for c in range(n_tiles):
        stage_ref[c] = y[:, c * LANES:(c + 1) * LANES]
    for r in range(dil):
        for c in range(n_tiles):
            lo = r * d + c * LANES
            dst_ref[:, lo:lo + LANES] = (
                stage_ref[c, pl.ds(r, rows // dil, stride=dil), :].astype(dst_ref.dtype))


def _proj_blocks(hn, w_ref, n_normed, gain_ref, e_red, e_exp):
    d = hn.shape[1]
    n_blk = w_ref.shape[1] // d
    ys = [jnp.dot(hn, w_ref[:, b * d:(b + 1) * d], preferred_element_type=F32)
          for b in range(n_blk)]
    ms = [_head_sum(ys[b] * ys[b], e_red) * (1.0 / HEAD_DIM) for b in range(n_normed)]
    sc = [_spread(lax.rsqrt(ms[b] + RMS_EPS), e_exp) for b in range(n_normed)]
    for b in range(n_normed):
        ys[b] = ys[b] * sc[b] * gain_ref[b]
    return ys


def _kv_proj_kernel(x_ref, gn_ref, w_ref, gain_ref, ered_ref, eexp_ref, o0_ref, o1_ref, o2_ref,
                    *stage_refs):
    n_groups = len(DIL_GROUPS)
    hn = _rms(x_ref[...], gn_ref[...]).astype(BF16)
    ys = _proj_blocks(hn, w_ref, n_groups, gain_ref, ered_ref[...], eexp_ref[...])
    for grp, out_ref in enumerate((o0_ref, o1_ref, o2_ref)):
        for is_value in range(2):
            _store_dilated(out_ref.at[is_value], ys[is_value * n_groups + grp],
                           stage_refs[grp], DIL_GROUPS[grp][1])


def _kv_proj(x, gn, w_kv, gains, e_red, e_exp):
    t, d = x.shape
    tm = TOKEN_TILE
    return pl.pallas_call(
        _kv_proj_kernel,
        grid=(t // tm,),
        in_specs=[pl.BlockSpec((tm, d), lambda i: (i, 0)), _full(gn.shape),
                  _resident(w_kv.shape), _full(gains.shape), _full(e_red.shape),
                  _full(e_exp.shape)],
        out_specs=[pl.BlockSpec((2, tm // dil, dil * d), lambda i: (0, i, 0))
                   for _, dil in DIL_GROUPS],
        out_shape=[jax.ShapeDtypeStruct((2, t // dil, dil * d), BF16) for _, dil in DIL_GROUPS],
        scratch_shapes=[pltpu.VMEM((d // LANES, tm, LANES), F32) for _ in DIL_GROUPS],
        compiler_params=_params("parallel"),
        name="kv_proj",
    )(x, gn, w_kv, gains, e_red, e_exp)


def _q_proj_kernel(x_ref, gn_ref, w_ref, gain_ref, ered_ref, eexp_ref, o0_ref, o1_ref, o2_ref,
                   *stage_refs):
    hn = _rms(x_ref[...], gn_ref[...]).astype(BF16)
    ys = _proj_blocks(hn, w_ref, len(DIL_GROUPS), gain_ref, ered_ref[...], eexp_ref[...])
    scale = HEAD_DIM ** -0.5 * LOG2_E
    for grp, out_ref in enumerate((o0_ref, o1_ref, o2_ref)):
        _store_dilated(out_ref, ys[grp] * scale, stage_refs[grp], DIL_GROUPS[grp][1])


def _q_proj(x, gn, w_q, gains, e_red, e_exp):
    t, d = x.shape
    tm = TOKEN_TILE
    return pl.pallas_call(
        _q_proj_kernel,
        grid=(t // tm,),
        in_specs=[pl.BlockSpec((tm, d), lambda i: (i, 0)), _full(gn.shape),
                  _resident(w_q.shape), _full(gains.shape), _full(e_red.shape),
                  _full(e_exp.shape)],
        out_specs=[pl.BlockSpec((tm // dil, dil * d), lambda i: (i, 0)) for _, dil in DIL_GROUPS],
        out_shape=[jax.ShapeDtypeStruct((t // dil, dil * d), BF16) for _, dil in DIL_GROUPS],
        scratch_shapes=[pltpu.VMEM((d // LANES, tm, LANES), F32) for _ in DIL_GROUPS],
        compiler_params=_params("parallel"),
        name="q_proj",
    )(x, gn, w_q, gains, e_red, e_exp)


def _dil_attn_kernel(q_ref, kp_ref, kc_ref, vp_ref, vc_ref, o_ref, m_ref, l_ref):
    n = pl.program_id(1)
    blk = ATT_BLK
    n_pairs = q_ref.shape[1] // LANES
    n_sub = q_ref.shape[0] // blk
    qi = lax.broadcasted_iota(jnp.int32, (blk, 2 * blk), 0)
    kj = lax.broadcasted_iota(jnp.int32, (blk, 2 * blk), 1)
    band = (kj >= qi) & (kj <= qi + blk)
    valid = [band & ((n > 0) | (kj >= blk))] + [band] * (n_sub - 1)
    lane = lax.broadcasted_iota(jnp.int32, (blk, LANES), 1)
    head0 = lane < HEAD_DIM
    dn_nt = (((1,), (1,)), ((), ()))
    zero = jnp.zeros((blk, LANES), BF16)
    lanes_of = lambda a, p: a[:, p * LANES:(p + 1) * LANES]

    def keys_of(prev_ref, cur_ref, s, p):
        if s == 0:
            return jnp.concatenate([lanes_of(prev_ref, p), lanes_of(cur_ref, p)[:blk]], axis=0)
        return lanes_of(cur_ref, p)[(s - 1) * blk:(s + 1) * blk]

    items = [(s, p, hh) for s in range(n_sub) for p in range(n_pairs) for hh in range(2)]
    scores = {}
    for s, p, hh in items:
        qp = lanes_of(q_ref, p)[s * blk:(s + 1) * blk]
        qm = jnp.where(head0, qp, zero) if hh == 0 else jnp.where(head0, zero, qp)
        scores[s, p, hh] = lax.dot_general(qm, keys_of(kp_ref, kc_ref, s, p), dn_nt,
                                           preferred_element_type=F32)
    probs = {}
    m_acc = [jnp.zeros((blk, LANES), F32) for _ in range(n_sub)]
    l_acc = [jnp.ones((blk, LANES), F32) for _ in range(n_sub)]
    for s, p, hh in items:
        sc = jnp.where(valid[s], scores.pop((s, p, hh)), NEG_INF)
        mx = jnp.max(sc, axis=-1, keepdims=True)
        pe = jnp.exp2(sc - mx)
        hit = lane == 2 * p + hh
        m_acc[s] = jnp.where(hit, mx, m_acc[s])
        l_acc[s] = jnp.where(hit, jnp.sum(pe, axis=-1, keepdims=True), l_acc[s])
        probs[s, p, hh] = pe.astype(BF16)
    for s in range(n_sub):
        for p in range(n_pairs):
            vcat = keys_of(vp_ref, vc_ref, s, p)
            o0 = jnp.dot(probs.pop((s, p, 0)), vcat, preferred_element_type=F32)
            o1 = jnp.dot(probs.pop((s, p, 1)), vcat, preferred_element_type=F32)
            o_ref[s * blk:(s + 1) * blk, p * LANES:(p + 1) * LANES] = jnp.where(head0, o0, o1)
        m_ref[s * blk:(s + 1) * blk, :] = m_acc[s]
        l_ref[s * blk:(s + 1) * blk, :] = l_acc[s]


def _dil_attn(q, kv, dil):
    m_rows, wide = q.shape
    d = wide // dil
    blk = ATT_BLK
    step = ATT_SUB * blk
    cur = lambda a: pl.BlockSpec((None, step, d), lambda r, n: (a, n, r))
    prev = lambda a: pl.BlockSpec((None, blk, d),
                                  lambda r, n: (a, jnp.maximum(ATT_SUB * n - 1, 0), r))
    return pl.pallas_call(
        _dil_attn_kernel,
        grid=(dil, m_rows // step),
        in_specs=[pl.BlockSpec((step, d), lambda r, n: (n, r)), prev(0), cur(0), prev(1), cur(1)],
        out_specs=[pl.BlockSpec((step, d), lambda r, n: (n, r)),
                   pl.BlockSpec((step, LANES), lambda r, n: (n, r)),
                   pl.BlockSpec((step, LANES), lambda r, n: (n, r))],
        out_shape=[jax.ShapeDtypeStruct((m_rows, dil * d), F32),
                   jax.ShapeDtypeStruct((m_rows, dil * LANES), F32),
                   jax.ShapeDtypeStruct((m_rows, dil * LANES), F32)],
        compiler_params=_params("parallel", "parallel"),
        name="dil_attn",
    )(q, kv, kv, kv, kv)


def _load_dilated(src_ref, stage_ref, dil):
    if dil == 1:
        return src_ref[...]
    sub = src_ref.shape[0]
    n_tiles = src_ref.shape[1] // dil // LANES
    for r in range(dil):
        for c in range(n_tiles):
            lo = (r * n_tiles + c) * LANES
            stage_ref[c, pl.ds(r, sub, stride=dil), :] = src_ref[:, lo:lo + LANES]
    return jnp.concatenate([stage_ref[c] for c in range(n_tiles)], axis=1)


def _attn_combine_kernel(o0, o1, o2, m0, m1, m2, l0, l1, l2, x_ref, wo_ref, eexp_ref, out_ref,
                         os1, os2, ms1, ms2, ls1, ls2):
    dils = [dil for _, dil in DIL_GROUPS]
    ms = [_load_dilated(ref, st, dil) for ref, st, dil in zip((m0, m1, m2), (None, ms1, ms2), dils)]
    ls = [_load_dilated(ref, st, dil) for ref, st, dil in zip((l0, l1, l2), (None, ls1, ls2), dils)]
    top = jnp.maximum(jnp.maximum(ms[0], ms[1]), ms[2])
    es = [jnp.exp2(mg - top) for mg in ms]
    den = es[0] * ls[0] + es[1] * ls[1] + es[2] * ls[2]
    e_exp = eexp_ref[...]
    acc = None
    for eg, ref, st, dil in zip(es, (o0, o1, o2), (None, os1, os2), dils):
        term = _spread(eg / den, e_exp) * _load_dilated(ref, st, dil)
        acc = term if acc is None else acc + term
    out_ref[...] = x_ref[...] + _mm(acc, wo_ref[...])


def _attn_combine(parts, x, w_o, e_exp):
    t, d = x.shape
    tm = TOKEN_TILE
    row = pl.BlockSpec((tm, d), lambda i: (i, 0))
    dils = [dil for _, dil in DIL_GROUPS]
    o_specs = [pl.BlockSpec((tm // dil, dil * d), lambda i: (i, 0)) for dil in dils]
    s_specs = [pl.BlockSpec((tm // dil, dil * LANES), lambda i: (i, 0)) for dil in dils]
    os_, ms_, ls_ = zip(*parts)
    return pl.pallas_call(
        _attn_combine_kernel,
        grid=(t // tm,),
        in_specs=o_specs + s_specs + s_specs + [row, _resident(w_o.shape), _full(e_exp.shape)],
        out_specs=row,
        out_shape=jax.ShapeDtypeStruct((t, d), F32),
        scratch_shapes=([pltpu.VMEM((d // LANES, tm, LANES), F32)] * 2
                        + [pltpu.VMEM((1, tm, LANES), F32)] * 4),
        compiler_params=_params("parallel"),
        name="attn_combine",
    )(*os_, *ms_, *ls_, x, w_o, e_exp)


def kernel(x, ffn_norm, ffn_w_in, ffn_w_out, mix_norm, rwkv_mu, rwkv_w_rkv, rwkv_w0, rwkv_w1, rwkv_w2, rwkv_a0, rwkv_a1, rwkv_a2, rwkv_v0, rwkv_v1, rwkv_v2, rwkv_g1, rwkv_g2, rwkv_k_k, rwkv_k_a, rwkv_r_k, rwkv_ln_w, rwkv_ln_b, rwkv_w_o, kv_norm, w_kv, k_norm, attn_w_q, q_norm, attn_w_o):
    bsz, t, d = x.shape
    assert bsz == 1 and d % LANES == 0 and t % max(w for w, _ in DIL_GROUPS) == 0
    assert all(w // dil == ATT_BLK for w, dil in DIL_GROUPS)
    depth = ffn_norm.shape[0]
    n_a = rwkv_mu.shape[0]
    n_groups = len(DIL_GROUPS)
    row = lambda p: p.reshape(1, -1)

    ffn_g = ffn_norm.reshape(2 * depth, 1, d)
    ffn_wi = ffn_w_in.astype(BF16).reshape(2 * depth, d, -1)
    ffn_wo = ffn_w_out.astype(BF16).reshape(2 * depth, -1, d)
    head_of_lane = jnp.arange(d) // HEAD_DIM
    e_exp = (jnp.arange(LANES)[:, None] == head_of_lane[None, :]).astype(BF16)
    e_red = e_exp.T

    xs = x.reshape(t, d)
    v_first = None
    kvs = None
    for l in range(depth):
        xs = _ffn(xs, ffn_g, ffn_wi, ffn_wo, 2 * l)
        if l < n_a:
            vres = None
            if l > 0:
                vres = (row(rwkv_v0[l - 1]), rwkv_v1[l - 1].astype(BF16),
                        rwkv_v2[l - 1].astype(BF16), v_first)
            r, ld, k, v, na, nb, g = _rwkv_proj(
                xs, row(mix_norm[l]), rwkv_mu[l], rwkv_w_rkv[l].astype(BF16), row(rwkv_w0[l]),
                rwkv_w1[l].astype(BF16), rwkv_w2[l].astype(BF16), row(rwkv_a0[l]),
                rwkv_a1[l].astype(BF16), rwkv_a2[l].astype(BF16), rwkv_g1[l].astype(BF16),
                rwkv_g2[l].astype(BF16), row(rwkv_k_k[l]), row(rwkv_k_a[l]), e_red, e_exp, vres)
            if l == 0:
                v_first = v
            y = _rwkv_scan(r, ld, k, v, na, nb)
            xs = _rwkv_post(y, r, k, v, g, xs, row(rwkv_ln_w[l]), row(rwkv_ln_b[l]),
                            row(rwkv_r_k[l]), rwkv_w_o[l].astype(BF16), e_red, e_exp)
        else:
            i = l - n_a
            q_gain = jnp.tile(q_norm[i], (1, d // HEAD_DIM)).reshape(n_groups, 1, d)
            qs = _q_proj(xs, row(mix_norm[l]), attn_w_q[i].astype(BF16), q_gain, e_red, e_exp)
            parts = [_dil_attn(qs[grp], kvs[grp], DIL_GROUPS[grp][1]) for grp in range(n_groups)]
            xs = _attn_combine(parts, xs, attn_w_o[i].astype(BF16), e_exp)
        xs = _ffn(xs, ffn_g, ffn_wi, ffn_wo, 2 * l + 1)
        if l == n_a - 1:
            k_gain = jnp.tile(k_norm, (1, d // HEAD_DIM)).reshape(n_groups, 1, d)
            kvs = _kv_proj(xs, row(kv_norm), w_kv.astype(BF16), k_gain, e_red, e_exp)
    return xs.reshape(bsz, t, d)
```

```python
import functools

import jax
import jax.numpy as jnp
from jax import lax
from jax.experimental import pallas as pl
from jax.experimental.pallas import tpu as pltpu

F32 = jnp.float32
BF16 = jnp.bfloat16

RMS_EPS = 1e-6
GN_EPS = 64e-5
HEAD_DIM = 64
LANES = 128
DIL_GROUPS = ((128, 1), (512, 4), (2048, 16))
ATT_BLK = 128
ATT_SUB = 2
ATT_WAVE = 1
NEG_INF = -1e30
LOG2_E = 1.4426950408889634
CHUNK = 64
SCAN_BLOCK = 256
TOKEN_TILE = 512
FFN_COL_TILE = 256
VMEM_LIMIT = 56 * 1024 * 1024


def _params(*sem):
    return pltpu.CompilerParams(dimension_semantics=sem, vmem_limit_bytes=VMEM_LIMIT)


def _mm(a, b):
    return jnp.dot(a.astype(BF16), b.astype(BF16), preferred_element_type=F32)


def _mm_nt(a, b):
    return lax.dot_general(a.astype(BF16), b.astype(BF16), (((1,), (1,)), ((), ())),
                           preferred_element_type=F32)


def _rms(x, g):
    return x * lax.rsqrt(jnp.mean(x * x, axis=-1, keepdims=True) + RMS_EPS) * g


def _sigmoid(x):
    return 1.0 / (1.0 + jnp.exp(-x))


def _full(shape):
    n = len(shape)
    return pl.BlockSpec(shape, lambda *_: (0,) * n)


def _resident(shape):
    n = len(shape)
    return pl.BlockSpec(shape, lambda *_: (0,) * n, pipeline_mode=pl.Buffered(1))


def _ffn_kernel(x_ref, g_ref, win_ref, wout_ref, o_ref, act_ref):
    x = x_ref[...]
    hn = _rms(x, g_ref[...]).astype(BF16)
    d_ff = wout_ref.shape[0]
    for c in range(d_ff // FFN_COL_TILE):
        lo = c * FFN_COL_TILE
        gate = jnp.dot(hn, win_ref[:, lo:lo + FFN_COL_TILE], preferred_element_type=F32)
        up = jnp.dot(hn, win_ref[:, d_ff + lo:d_ff + lo + FFN_COL_TILE],
                     preferred_element_type=F32)
        act_ref[:, lo:lo + FFN_COL_TILE] = (gate * _sigmoid(gate) * up).astype(BF16)
    o_ref[...] = x + 0.5 * jnp.dot(act_ref[...], wout_ref[...], preferred_element_type=F32)


def _ffn(x, norms, w_in, w_out, idx):
    t, d = x.shape
    d_ff = w_out.shape[1]
    tm = TOKEN_TILE
    return pl.pallas_call(
        _ffn_kernel,
        grid=(t // tm,),
        in_specs=[
            pl.BlockSpec((tm, d), lambda i: (i, 0)),
            pl.BlockSpec((None, 1, d), lambda i: (idx, 0, 0)),
            pl.BlockSpec((None, d, 2 * d_ff), lambda i: (idx, 0, 0), pipeline_mode=pl.Buffered(1)),
            pl.BlockSpec((None, d_ff, d), lambda i: (idx, 0, 0), pipeline_mode=pl.Buffered(1)),
        ],
        out_specs=pl.BlockSpec((tm, d), lambda i: (i, 0)),
        out_shape=jax.ShapeDtypeStruct((t, d), F32),
        scratch_shapes=[pltpu.VMEM((tm, d_ff), BF16)],
        compiler_params=_params("parallel"),
        name="ffn",
    )(x, norms, w_in, w_out)


def _head_sum(x, e_red):
    return jnp.dot(x.astype(BF16), e_red, preferred_element_type=F32)


def _spread(c, e_exp):
    hi = c.astype(BF16)
    lo = (c - hi.astype(F32)).astype(BF16)
    return jnp.dot(jnp.concatenate([hi, lo], axis=1), e_exp, preferred_element_type=F32)


def _rwkv_proj_kernel(*refs, has_vres):
    if has_vres:
        (x_ref, xp_ref, gn_ref, mu_ref, wrkv_ref, w0_ref, w1_ref, w2_ref, a0_ref, a1_ref,
         a2_ref, g1_ref, g2_ref, kk_ref, ka_ref, ered_ref, eexp_ref, v0_ref, v1_ref, v2_ref,
         vf_ref, r_o, ld_o, k_o, v_o, na_o, nb_o, g_o) = refs
    else:
        (x_ref, xp_ref, gn_ref, mu_ref, wrkv_ref, w0_ref, w1_ref, w2_ref, a0_ref, a1_ref,
         a2_ref, g1_ref, g2_ref, kk_ref, ka_ref, ered_ref, eexp_ref,
         r_o, ld_o, k_o, v_o, na_o, nb_o, g_o) = refs
    i = pl.program_id(0)
    gn = gn_ref[...]
    h = _rms(x_ref[...], gn)
    hp = _rms(xp_ref[7:8, :], gn) * jnp.where(i > 0, 1.0, 0.0)
    rows = lax.broadcasted_iota(jnp.int32, h.shape, 0)
    hs = jnp.where(rows == 0, hp, pltpu.roll(h, 1, axis=0))
    xx = hs - h
    mu = mu_ref[...]
    xr, xw, xk, xv, xa, xg = [h + xx * mu[j:j + 1] for j in range(6)]
    w_dn = _mm(xw, w1_ref[...])
    a_dn = _mm(xa, a1_ref[...])
    g_dn = _mm(xg, g1_ref[...])
    v_dn = _mm(xv, v1_ref[...]) if has_vres else None
    r = _mm(xr, wrkv_ref[0])
    k = _mm(xk, wrkv_ref[1])
    v = _mm(xv, wrkv_ref[2])
    w = w0_ref[...] + _mm(jnp.tanh(w_dn), w2_ref[...])
    z = -w
    wl = -(jnp.maximum(z, 0.0) + jnp.log(1.0 + jnp.exp(-jnp.abs(z)))) - 0.5
    ld_o[...] = -jnp.exp(wl)
    a = _sigmoid(a0_ref[...] + _mm(a_dn, a2_ref[...]))
    g_o[...] = _mm(_sigmoid(g_dn), g2_ref[...])
    if has_vres:
        v = v + (vf_ref[...] - v) * _sigmoid(v0_ref[...] + _mm(v_dn, v2_ref[...]))
    kkr = k * kk_ref[...]
    norm = jnp.sqrt(_head_sum(kkr * kkr, ered_ref[...]))
    kk = kkr * _spread(1.0 / jnp.maximum(norm, 1e-12), eexp_ref[...])
    r_o[...] = r
    k_o[...] = k * (1.0 + (a - 1.0) * ka_ref[...])
    v_o[...] = v
    na_o[...] = -kk
    nb_o[...] = kk * a


def _rwkv_proj(x, gn, mu, wrkv, w0, w1, w2, a0, a1, a2, g1, g2, k_k, k_a, e_red, e_exp, vres):
    t, d = x.shape
    tm = TOKEN_TILE // 2
    row = pl.BlockSpec((tm, d), lambda i: (i, 0))
    prev = pl.BlockSpec((8, d), lambda i: (jnp.maximum(i * (tm // 8) - 1, 0), 0))
    args = [x, x, gn, mu, wrkv, w0, w1, w2, a0, a1, a2, g1, g2, k_k, k_a, e_red, e_exp]
    specs = [row, prev, _full(gn.shape), _full(mu.shape), _resident(wrkv.shape), _full(w0.shape),
             _full(w1.shape), _full(w2.shape), _full(a0.shape), _full(a1.shape), _full(a2.shape),
             _full(g1.shape), _full(g2.shape), _full(k_k.shape), _full(k_a.shape),
             _full(e_red.shape), _full(e_exp.shape)]
    if vres is not None:
        v0, v1, v2, v_first = vres
        args += [v0, v1, v2, v_first]
        specs += [_full(v0.shape), _full(v1.shape), _full(v2.shape), row]
    return pl.pallas_call(
        functools.partial(_rwkv_proj_kernel, has_vres=vres is not None),
        grid=(t // tm,),
        in_specs=specs,
        out_specs=[row] * 7,
        out_shape=[jax.ShapeDtypeStruct((t, d), F32)] * 7,
        compiler_params=_params("parallel"),
        name="rwkv_proj",
    )(*args)


def _interleave(primary, filler):
    out = []
    for i in range(max(len(primary), len(filler))):
        if i < len(filler):
            out.append(filler[i])
        if i < len(primary):
            out.append(primary[i])
    return out


def _rwkv_scan_kernel(r_ref, ld_ref, k_ref, v_ref, na_ref, nb_ref, y_ref, s_ref):
    n_pairs = r_ref.shape[1] // LANES
    n_chunks = r_ref.shape[0] // CHUNK
    L = CHUNK
    assert 2 * L == LANES and L == HEAD_DIM and n_chunks % 2 == 0

    @pl.when(pl.program_id(0) == 0)
    def _():
        s_ref[...] = jnp.zeros_like(s_ref)

    ri = lax.broadcasted_iota(jnp.int32, (2 * L, 2 * L), 0)
    ci = lax.broadcasted_iota(jnp.int32, (2 * L, 2 * L), 1)
    same_head = (ri < L) == (ci < L)
    other_head = jnp.logical_not(same_head)
    strict = (ci & (L - 1)) < (ri & (L - 1))
    incl = (ci & (L - 1)) <= (ri & (L - 1))
    eye = (ri == ci).astype(F32)
    head0 = lax.broadcasted_iota(jnp.int32, (L, LANES), 1) < HEAD_DIM
    first_head = lax.broadcasted_iota(jnp.int32, (2 * L, LANES), 1) < HEAD_DIM
    tri = (lax.broadcasted_iota(jnp.int32, (L, L), 0)
           >= lax.broadcasted_iota(jnp.int32, (L, L), 1)).astype(BF16)
    zero = jnp.zeros((2 * L, LANES), BF16)
    pairs = range(n_pairs)
    lanes_of = lambda a, p: a[:, p * LANES:(p + 1) * LANES]
    stack2 = lambda a, b: jnp.concatenate([a, b], axis=0)

    state = [s_ref[p] for p in pairs]

    def prep_stages(chunks):
        items = [(c, p) for c in chunks for p in pairs]
        d = {}

        def elementwise():
            for c in chunks:
                rows = slice(c * L, (c + 1) * L)
                r, ld, k = r_ref[rows, :], ld_ref[rows, :], k_ref[rows, :]
                v, na, nb = v_ref[rows, :], na_ref[rows, :], nb_ref[rows, :]
                l1 = ld.astype(BF16)
                rem = ld - l1.astype(F32)
                l2 = rem.astype(BF16)
                l3 = (rem - l2.astype(F32)).astype(BF16)
                tri_dot = lambda a: jnp.dot(tri, a, preferred_element_type=F32)
                cum = tri_dot(l1) + (tri_dot(l2) + tri_dot(l3))
                cum_last = cum[L - 1:L, :]
                w_inv = jnp.exp(-cum)
                w_rel = jnp.exp(cum_last - cum)
                d["w_last", c] = jnp.exp(cum_last)
                d["x", c] = stack2(r * jnp.exp(cum), na * jnp.exp(cum - ld)).astype(BF16)
                d["bk", c] = stack2(nb * w_inv, k * w_inv).astype(BF16)
                d["kb", c] = stack2(k * w_inv, nb * w_inv).astype(BF16)
                d["keys", c] = stack2(nb * w_rel, k * w_rel).astype(BF16)
                d["vv", c] = stack2(v, v).astype(BF16)
                d["v", c] = v

        def gram():
            for c, p in items:
                rhs = stack2(jnp.where(first_head, lanes_of(d["bk", c], p), zero),
                             jnp.where(first_head, zero, lanes_of(d["kb", c], p)))
                d["g", c, p] = _mm_nt(lanes_of(d["x", c], p), rhs)

        def split():
            for c, p in items:
                g = d.pop(("g", c, p))
                g0, g1 = g[:, :LANES], g[:, LANES:]
                qs = stack2(g0[L:], g1[L:])
                ms = stack2(g0[:L], g1[:L])
                nil = jnp.where(same_head & strict, qs, 0.0)
                d["m_uv", c, p] = jnp.concatenate(
                    [jnp.where(same_head & incl, ms, 0.0),
                     jnp.where(other_head & incl, ms, 0.0)], axis=1).astype(BF16)
                d["akv", c, p] = _mm(jnp.where(other_head & strict, qs, 0.0),
                                     lanes_of(d["vv", c], p))
                d["t", c, p] = eye + nil
                d["pw", c, p] = _mm(nil, nil)

        def double():
            for c, p in items:
                pw = d["pw", c, p]
                t = d["t", c, p]
                both = _mm(pw, jnp.concatenate([t, pw], axis=1))
                d["t", c, p] = t + both[:, :LANES]
                d["pw", c, p] = both[:, LANES:]

        def finish():
            for c, p in items:
                t = d["t", c, p]
                d["t", c, p] = (t + _mm(d.pop(("pw", c, p)), t)).astype(BF16)

        return [elementwise, gram, split, double, double, double, double, finish], d

    def chunk_stages(c, d):
        loc = {}

        def project():
            for p in pairs:
                loc["xs", p] = _mm_nt(lanes_of(d["x", c], p), state[p])

        def solve():
            for p in pairs:
                xs = loc["xs", p]
                u = _mm(d["t", c, p], stack2(xs[L:], xs[L:]) + d["akv", c, p])
                loc["u", p] = jnp.where(head0, u[:L], u[L:])

        def emit():
            for p in pairs:
                xs, u = loc["xs", p], loc["u", p]
                ub = u.astype(BF16)
                vvp = lanes_of(d["vv", c], p)
                y = stack2(xs[:L], xs[:L]) + _mm(d["m_uv", c, p],
                                                 jnp.concatenate([ub, ub, vvp], axis=0))
                y_ref[c * L:(c + 1) * L, p * LANES:(p + 1) * LANES] = jnp.where(
                    head0, y[:L], y[L:])
                vals = stack2(u, lanes_of(d["v", c], p))
                upd = _mm(vals.T, lanes_of(d["keys", c], p))
                state[p] = jnp.where(
                    same_head, state[p] * lanes_of(d["w_last", c], p) + upd, 0.0)

        return [project, solve, emit]

    groups = [(c, c + 1) for c in range(0, n_chunks, 2)]
    prep, data = prep_stages(groups[0])
    for stage in prep:
        stage()
    for gi, grp in enumerate(groups):
        chain = chunk_stages(grp[0], data) + chunk_stages(grp[1], data)
        if gi + 1 < len(groups):
            prep, nxt = prep_stages(groups[gi + 1])
        else:
            prep, nxt = [], None
        for stage in _interleave(chain, prep):
            stage()
        data = nxt
    for p in pairs:
        s_ref[p] = state[p]


def _rwkv_scan(r, ld, k, v, na, nb):
    t, d = r.shape
    blk = pl.BlockSpec((SCAN_BLOCK, d), lambda c: (c, 0))
    return pl.pallas_call(
        _rwkv_scan_kernel,
        grid=(t // SCAN_BLOCK,),
        in_specs=[blk] * 6,
        out_specs=blk,
        out_shape=jax.ShapeDtypeStruct((t, d), F32),
        scratch_shapes=[pltpu.VMEM((d // LANES, LANES, LANES), F32)],
        compiler_params=_params("arbitrary"),
        name="rwkv_scan",
    )(r, ld, k, v, na, nb)


def _rwkv_post_kernel(y_ref, r_ref, k_ref, v_ref, g_ref, x_ref, lnw_ref, lnb_ref, rk_ref,
                      wo_ref, ered_ref, eexp_ref, o_ref):
    e_red = ered_ref[...]
    e_exp = eexp_ref[...]
    inv_n = 1.0 / HEAD_DIM
    n_sub = 4
    sub = y_ref.shape[0] // n_sub
    blocks = [slice(s * sub, (s + 1) * sub) for s in range(n_sub)]
    ys = [y_ref[b, :] for b in blocks]
    sum_y = [_head_sum(y, e_red) for y in ys]
    sum_b = [_head_sum(r_ref[b, :] * k_ref[b, :] * rk_ref[...], e_red) for b in blocks]
    yc = [y - _spread(s * inv_n, e_exp) for y, s in zip(ys, sum_y)]
    bonus = [_spread(s, e_exp) * v_ref[b, :] for s, b in zip(sum_b, blocks)]
    var = [_head_sum(c * c, e_red) * inv_n for c in yc]
    rstd = [_spread(lax.rsqrt(v + GN_EPS), e_exp) for v in var]
    z = [((c * rs * lnw_ref[...] + lnb_ref[...]) + bn) * g_ref[b, :]
         for c, rs, bn, b in zip(yc, rstd, bonus, blocks)]
    o_ref[...] = x_ref[...] + _mm(jnp.concatenate(z, axis=0), wo_ref[...])


def _rwkv_post(y, r, k, v, g, x, ln_w, ln_b, r_k, w_o, e_red, e_exp):
    t, d = x.shape
    tm = TOKEN_TILE
    row = pl.BlockSpec((tm, d), lambda i: (i, 0))
    return pl.pallas_call(
        _rwkv_post_kernel,
        grid=(t // tm,),
        in_specs=[row] * 6 + [_full(ln_w.shape), _full(ln_b.shape), _full(r_k.shape),
                              _resident(w_o.shape), _full(e_red.shape), _full(e_exp.shape)],
        out_specs=row,
        out_shape=jax.ShapeDtypeStruct((t, d), F32),
        compiler_params=_params("parallel"),
        name="rwkv_post",
    )(y, r, k, v, g, x, ln_w, ln_b, r_k, w_o, e_red, e_exp)


def _store_dilated(dst_ref, y, stage_ref, dil):
    rows, d = y.shape
    if dil == 1:
        dst_ref[...] = y.astype(dst_ref.dtype)
        return
    n_tiles = d // LANES
    for c in range(n_tiles):
        stage_ref[c] = y[:, c * LANES:(c + 1) * LANES]
    for r in range(dil):
        for c in range(n_tiles):
            lo = r * d + c * LANES
            dst_ref[:, lo:lo + LANES] = (
                stage_ref[c, pl.ds(r, rows // dil, stride=dil), :].astype(dst_ref.dtype))


def _proj_blocks(hn, w_ref, n_normed, gain_ref, e_red, e_exp):
    d = hn.shape[1]
    n_blk = w_ref.shape[1] // d
    ys = [jnp.dot(hn, w_ref[:, b * d:(b + 1) * d], preferred_element_type=F32)
          for b in range(n_blk)]
    ms = [_head_sum(ys[b] * ys[b], e_red) * (1.0 / HEAD_DIM) for b in range(n_normed)]
    sc = [_spread(lax.rsqrt(ms[b] + RMS_EPS), e_exp) for b in range(n_normed)]
    for b in range(n_normed):
        ys[b] = ys[b] * sc[b] * gain_ref[b]
    return ys


def _kv_proj_kernel(x_ref, gn_ref, w_ref, gain_ref, ered_ref, eexp_ref, o0_ref, o1_ref, o2_ref,
                    *stage_refs):
    n_groups = len(DIL_GROUPS)
    hn = _rms(x_ref[...], gn_ref[...]).astype(BF16)
    ys = _proj_blocks(hn, w_ref, n_groups, gain_ref, ered_ref[...], eexp_ref[...])
    for grp, out_ref in enumerate((o0_ref, o1_ref, o2_ref)):
        for is_value in range(2):
            _store_dilated(out_ref.at[is_value], ys[is_value * n_groups + grp],
                           stage_refs[grp], DIL_GROUPS[grp][1])


def _kv_proj(x, gn, w_kv, gains, e_red, e_exp):
    t, d = x.shape
    tm = TOKEN_TILE
    return pl.pallas_call(
        _kv_proj_kernel,
        grid=(t // tm,),
        in_specs=[pl.BlockSpec((tm, d), lambda i: (i, 0)), _full(gn.shape),
                  _resident(w_kv.shape), _full(gains.shape), _full(e_red.shape),
                  _full(e_exp.shape)],
        out_specs=[pl.BlockSpec((2, tm // dil, dil * d), lambda i: (0, i, 0))
                   for _, dil in DIL_GROUPS],
        out_shape=[jax.ShapeDtypeStruct((2, t // dil, dil * d), BF16) for _, dil in DIL_GROUPS],
        scratch_shapes=[pltpu.VMEM((d // LANES, tm, LANES), F32) for _ in DIL_GROUPS],
        compiler_params=_params("parallel"),
        name="kv_proj",
    )(x, gn, w_kv, gains, e_red, e_exp)


def _q_proj_kernel(x_ref, gn_ref, w_ref, gain_ref, ered_ref, eexp_ref, o0_ref, o1_ref, o2_ref,
                   *stage_refs):
    hn = _rms(x_ref[...], gn_ref[...]).astype(BF16)
    ys = _proj_blocks(hn, w_ref, len(DIL_GROUPS), gain_ref, ered_ref[...], eexp_ref[...])
    scale = HEAD_DIM ** -0.5 * LOG2_E
    for grp, out_ref in enumerate((o0_ref, o1_ref, o2_ref)):
        _store_dilated(out_ref, ys[grp] * scale, stage_refs[grp], DIL_GROUPS[grp][1])


def _q_proj(x, gn, w_q, gains, e_red, e_exp):
    t, d = x.shape
    tm = TOKEN_TILE
    return pl.pallas_call(
        _q_proj_kernel,
        grid=(t // tm,),
        in_specs=[pl.BlockSpec((tm, d), lambda i: (i, 0)), _full(gn.shape),
                  _resident(w_q.shape), _full(gains.shape), _full(e_red.shape),
                  _full(e_exp.shape)],
        out_specs=[pl.BlockSpec((tm // dil, dil * d), lambda i: (i, 0)) for _, dil in DIL_GROUPS],
        out_shape=[jax.ShapeDtypeStruct((t // dil, dil * d), BF16) for _, dil in DIL_GROUPS],
        scratch_shapes=[pltpu.VMEM((d // LANES, tm, LANES), F32) for _ in DIL_GROUPS],
        compiler_params=_params("parallel"),
        name="q_proj",
    )(x, gn, w_q, gains, e_red, e_exp)


def _dil_attn_kernel(q_ref, kp_ref, kc_ref, vp_ref, vc_ref, o_ref, m_ref, l_ref):
    n = pl.program_id(1)
    blk = ATT_BLK
    n_pairs = q_ref.shape[1] // LANES
    n_sub = q_ref.shape[0] // blk
    qi = lax.broadcasted_iota(jnp.int32, (2 * blk, 2 * blk), 0) & (blk - 1)
    kj = lax.broadcasted_iota(jnp.int32, (2 * blk, 2 * blk), 1)
    band = (kj >= qi) & (kj <= qi + blk)
    valid = [band & ((n > 0) | (kj >= blk))] + [band] * (n_sub - 1)
    lane = lax.broadcasted_iota(jnp.int32, (blk, LANES), 1)
    head0 = lane < HEAD_DIM
    dn_nt = (((1,), (1,)), ((), ()))
    zero = jnp.zeros((blk, LANES), BF16)
    lanes_of = lambda a, p: a[:, p * LANES:(p + 1) * LANES]

    def keys_of(prev_ref, cur_ref, s, p):
        if s == 0:
            return jnp.concatenate([lanes_of(prev_ref, p), lanes_of(cur_ref, p)[:blk]], axis=0)
        return lanes_of(cur_ref, p)[(s - 1) * blk:(s + 1) * blk]

    waves = [(s, range(p0, p0 + ATT_WAVE)) for s in range(n_sub)
             for p0 in range(0, n_pairs, ATT_WAVE)]
    scores, probs = {}, {}
    m_acc = [jnp.zeros((blk, LANES), F32) for _ in range(n_sub)]
    l_acc = [jnp.ones((blk, LANES), F32) for _ in range(n_sub)]

    def score_stage(s, ps):
        for p in ps:
            qp = lanes_of(q_ref, p)[s * blk:(s + 1) * blk]
            q2 = jnp.concatenate([jnp.where(head0, qp, zero), jnp.where(head0, zero, qp)], axis=0)
            scores[s, p] = lax.dot_general(q2, keys_of(kp_ref, kc_ref, s, p), dn_nt,
                                           preferred_element_type=F32)

    def softmax_stage(s, ps):
        for p in ps:
            sc = jnp.where(valid[s], scores.pop((s, p)), NEG_INF)
            mx = jnp.max(sc, axis=-1, keepdims=True)
            pe = jnp.exp2(sc - mx)
            ls = jnp.sum(pe, axis=-1, keepdims=True)
            for hh in range(2):
                hit = lane == 2 * p + hh
                rows = slice(hh * blk, (hh + 1) * blk)
                m_acc[s] = jnp.where(hit, mx[rows], m_acc[s])
                l_acc[s] = jnp.where(hit, ls[rows], l_acc[s])
            probs[s, p] = pe.astype(BF16)

    def output_stage(s, ps):
        for p in ps:
            o2 = jnp.dot(probs.pop((s, p)), keys_of(vp_ref, vc_ref, s, p),
                         preferred_element_type=F32)
            o_ref[s * blk:(s + 1) * blk, p * LANES:(p + 1) * LANES] = jnp.where(
                head0, o2[:blk], o2[blk:])

    for i in range(len(waves) + 2):
        for lag, stage in enumerate((score_stage, softmax_stage, output_stage)):
            if 0 <= i - lag < len(waves):
                stage(*waves[i - lag])
    for s in range(n_sub):
        m_ref[s * blk:(s + 1) * blk, :] = m_acc[s]
        l_ref[s * blk:(s + 1) * blk, :] = l_acc[s]


def _dil_attn(q, kv, dil):
    m_rows, wide = q.shape
    d = wide // dil
    blk = ATT_BLK
    step = ATT_SUB * blk
    cur = lambda a: pl.BlockSpec((None, step, d), lambda r, n: (a, n, r))
    prev = lambda a: pl.BlockSpec((None, blk, d),
                                  lambda r, n: (a, jnp.maximum(ATT_SUB * n - 1, 0), r))
    return pl.pallas_call(
        _dil_attn_kernel,
        grid=(dil, m_rows // step),
        in_specs=[pl.BlockSpec((step, d), lambda r, n: (n, r)), prev(0), cur(0), prev(1), cur(1)],
        out_specs=[pl.BlockSpec((step, d), lambda r, n: (n, r)),
                   pl.BlockSpec((step, LANES), lambda r, n: (n, r)),
                   pl.BlockSpec((step, LANES), lambda r, n: (n, r))],
        out_shape=[jax.ShapeDtypeStruct((m_rows, dil * d), F32),
                   jax.ShapeDtypeStruct((m_rows, dil * LANES), F32),
                   jax.ShapeDtypeStruct((m_rows, dil * LANES), F32)],
        compiler_params=_params("parallel", "parallel"),
        name="dil_attn",
    )(q, kv, kv, kv, kv)


def _load_dilated(src_ref, stage_ref, dil):
    if dil == 1:
        return src_ref[...]
    sub = src_ref.shape[0]
    n_tiles = src_ref.shape[1] // dil // LANES
    for r in range(dil):
        for c in range(n_tiles):
            lo = (r * n_tiles + c) * LANES
            stage_ref[c, pl.ds(r, sub, stride=dil), :] = src_ref[:, lo:lo + LANES]
    return jnp.concatenate([stage_ref[c] for c in range(n_tiles)], axis=1)


def _attn_combine_kernel(o0, o1, o2, m0, m1, m2, l0, l1, l2, x_ref, wo_ref, eexp_ref, out_ref,
                         os1, os2, ms1, ms2, ls1, ls2):
    dils = [dil for _, dil in DIL_GROUPS]
    ms = [_load_dilated(ref, st, dil) for ref, st, dil in zip((m0, m1, m2), (None, ms1, ms2), dils)]
    ls = [_load_dilated(ref, st, dil) for ref, st, dil in zip((l0, l1, l2), (None, ls1, ls2), dils)]
    top = jnp.maximum(jnp.maximum(ms[0], ms[1]), ms[2])
    es = [jnp.exp2(mg - top) for mg in ms]
    den = es[0] * ls[0] + es[1] * ls[1] + es[2] * ls[2]
    e_exp = eexp_ref[...]
    acc = None
    for eg, ref, st, dil in zip(es, (o0, o1, o2), (None, os1, os2), dils):
        term = _spread(eg / den, e_exp) * _load_dilated(ref, st, dil)
        acc = term if acc is None else acc + term
    out_ref[...] = x_ref[...] + _mm(acc, wo_ref[...])


def _attn_combine(parts, x, w_o, e_exp):
    t, d = x.shape
    tm = TOKEN_TILE
    row = pl.BlockSpec((tm, d), lambda i: (i, 0))
    dils = [dil for _, dil in DIL_GROUPS]
    o_specs = [pl.BlockSpec((tm // dil, dil * d), lambda i: (i, 0)) for dil in dils]
    s_specs = [pl.BlockSpec((tm // dil, dil * LANES), lambda i: (i, 0)) for dil in dils]
    os_, ms_, ls_ = zip(*parts)
    return pl.pallas_call(
        _attn_combine_kernel,
        grid=(t // tm,),
        in_specs=o_specs + s_specs + s_specs + [row, _resident(w_o.shape), _full(e_exp.shape)],
        out_specs=row,
        out_shape=jax.ShapeDtypeStruct((t, d), F32),
        scratch_shapes=([pltpu.VMEM((d // LANES, tm, LANES), F32)] * 2
                        + [pltpu.VMEM((1, tm, LANES), F32)] * 4),
        compiler_params=_params("parallel"),
        name="attn_combine",
    )(*os_, *ms_, *ls_, x, w_o, e_exp)


def kernel(x, ffn_norm, ffn_w_in, ffn_w_out, mix_norm, rwkv_mu, rwkv_w_rkv, rwkv_w0, rwkv_w1, rwkv_w2, rwkv_a0, rwkv_a1, rwkv_a2, rwkv_v0, rwkv_v1, rwkv_v2, rwkv_g1, rwkv_g2, rwkv_k_k, rwkv_k_a, rwkv_r_k, rwkv_ln_w, rwkv_ln_b, rwkv_w_o, kv_norm, w_kv, k_norm, attn_w_q, q_norm, attn_w_o):
    bsz, t, d = x.shape
    assert bsz == 1 and d % LANES == 0 and t % max(w for w, _ in DIL_GROUPS) == 0
    assert all(w // dil == ATT_BLK for w, dil in DIL_GROUPS)
    depth = ffn_norm.shape[0]
    n_a = rwkv_mu.shape[0]
    n_groups = len(DIL_GROUPS)
    row = lambda p: p.reshape(1, -1)

    ffn_g = ffn_norm.reshape(2 * depth, 1, d)
    ffn_wi = ffn_w_in.astype(BF16).reshape(2 * depth, d, -1)
    ffn_wo = ffn_w_out.astype(BF16).reshape(2 * depth, -1, d)
    head_of_lane = jnp.arange(d) // HEAD_DIM
    one_hot = (jnp.arange(LANES)[:, None] == head_of_lane[None, :]).astype(BF16)
    e_red = one_hot.T
    e_exp = jnp.concatenate([one_hot, one_hot], axis=0)

    xs = x.reshape(t, d)
    v_first = None
    kvs = None
    for l in range(depth):
        xs = _ffn(xs, ffn_g, ffn_wi, ffn_wo, 2 * l)
        if l < n_a:
            vres = None
            if l > 0:
                vres = (row(rwkv_v0[l - 1]), rwkv_v1[l - 1].astype(BF16),
                        rwkv_v2[l - 1].astype(BF16), v_first)
            r, ld, k, v, na, nb, g = _rwkv_proj(
                xs, row(mix_norm[l]), rwkv_mu[l], rwkv_w_rkv[l].astype(BF16), row(rwkv_w0[l]),
                rwkv_w1[l].astype(BF16), rwkv_w2[l].astype(BF16), row(rwkv_a0[l]),
                rwkv_a1[l].astype(BF16), rwkv_a2[l].astype(BF16), rwkv_g1[l].astype(BF16),
                rwkv_g2[l].astype(BF16), row(rwkv_k_k[l]), row(rwkv_k_a[l]), e_red, e_exp, vres)
            if l == 0:
                v_first = v
            y = _rwkv_scan(r, ld, k, v, na, nb)
            xs = _rwkv_post(y, r, k, v, g, xs, row(rwkv_ln_w[l]), row(rwkv_ln_b[l]),
                            row(rwkv_r_k[l]), rwkv_w_o[l].astype(BF16), e_red, e_exp)
        else:
            i = l - n_a
            q_gain = jnp.tile(q_norm[i], (1, d // HEAD_DIM)).reshape(n_groups, 1, d)
            qs = _q_proj(xs, row(mix_norm[l]), attn_w_q[i].astype(BF16), q_gain, e_red, e_exp)
            parts = [_dil_attn(qs[grp], kvs[grp], DIL_GROUPS[grp][1]) for grp in range(n_groups)]
            xs = _attn_combine(parts, xs, attn_w_o[i].astype(BF16), e_exp)
        xs = _ffn(xs, ffn_g, ffn_wi, ffn_wo, 2 * l + 1)
        if l == n_a - 1:
            k_gain = jnp.tile(k_norm, (1, d // HEAD_DIM)).reshape(n_groups, 1, d)
            kvs = _kv_proj(xs, row(kv_norm), w_kv.astype(BF16), k_gain, e_red, e_exp)
    return xs.reshape(bsz, t, d)
```

```python
import functools

import jax
import jax.numpy as jnp
from jax import lax
from jax.experimental import pallas as pl
from jax.experimental.pallas import tpu as pltpu

F32 = jnp.float32
BF16 = jnp.bfloat16

RMS_EPS = 1e-6
GN_EPS = 64e-5
HEAD_DIM = 64
LANES = 128
DIL_GROUPS = ((128, 1), (512, 4), (2048, 16))
ATT_BLK = 128
ATT_SUB = 2
ATT_WAVE = 1
NEG_INF = -1e30
LOG2_E = 1.4426950408889634
CHUNK = 64
SCAN_BLOCK = 256
TOKEN_TILE = 512
FFN_COL_TILE = 256
VMEM_LIMIT = 56 * 1024 * 1024


def _params(*sem):
    return pltpu.CompilerParams(dimension_semantics=sem, vmem_limit_bytes=VMEM_LIMIT)


def _mm(a, b):
    return jnp.dot(a.astype(BF16), b.astype(BF16), preferred_element_type=F32)


def _mm_nt(a, b):
    return lax.dot_general(a.astype(BF16), b.astype(BF16), (((1,), (1,)), ((), ())),
                           preferred_element_type=F32)


def _rms(x, g):
    return x * lax.rsqrt(jnp.mean(x * x, axis=-1, keepdims=True) + RMS_EPS) * g


def _sigmoid(x):
    return 1.0 / (1.0 + jnp.exp(-x))


def _full(shape):
    n = len(shape)
    return pl.BlockSpec(shape, lambda *_: (0,) * n)


def _resident(shape):
    n = len(shape)
    return pl.BlockSpec(shape, lambda *_: (0,) * n, pipeline_mode=pl.Buffered(1))


def _ffn_kernel(x_ref, g_ref, win_ref, wout_ref, o_ref, act_ref):
    x = x_ref[...]
    hn = _rms(x, g_ref[...]).astype(BF16)
    d_ff = wout_ref.shape[0]
    for c in range(d_ff // FFN_COL_TILE):
        lo = c * FFN_COL_TILE
        gate = jnp.dot(hn, win_ref[:, lo:lo + FFN_COL_TILE], preferred_element_type=F32)
        up = jnp.dot(hn, win_ref[:, d_ff + lo:d_ff + lo + FFN_COL_TILE],
                     preferred_element_type=F32)
        act_ref[:, lo:lo + FFN_COL_TILE] = (gate * _sigmoid(gate) * up).astype(BF16)
    o_ref[...] = x + 0.5 * jnp.dot(act_ref[...], wout_ref[...].astype(BF16),
                                   preferred_element_type=F32)


def _ffn(x, norms, w_in, w_out, idx):
    t, d = x.shape
    d_ff = w_out.shape[1]
    tm = TOKEN_TILE
    return pl.pallas_call(
        _ffn_kernel,
        grid=(t // tm,),
        in_specs=[
            pl.BlockSpec((tm, d), lambda i: (i, 0)),
            pl.BlockSpec((None, 1, d), lambda i: (idx, 0, 0)),
            pl.BlockSpec((None, d, 2 * d_ff), lambda i: (idx, 0, 0), pipeline_mode=pl.Buffered(1)),
            pl.BlockSpec((None, d_ff, d), lambda i: (idx, 0, 0), pipeline_mode=pl.Buffered(1)),
        ],
        out_specs=pl.BlockSpec((tm, d), lambda i: (i, 0)),
        out_shape=jax.ShapeDtypeStruct((t, d), F32),
        scratch_shapes=[pltpu.VMEM((tm, d_ff), BF16)],
        compiler_params=_params("parallel"),
        name="ffn",
    )(x, norms, w_in, w_out)


def _head_sum(x, e_red):
    return jnp.dot(x.astype(BF16), e_red, preferred_element_type=F32)


def _spread(c, e_exp):
    hi = c.astype(BF16)
    lo = (c - hi.astype(F32)).astype(BF16)
    return jnp.dot(jnp.concatenate([hi, lo], axis=1), e_exp, preferred_element_type=F32)


def _rwkv_proj_kernel(*refs, has_vres):
    if has_vres:
        (x_ref, xp_ref, gn_ref, mu_ref, wrkv_ref, w0_ref, w1_ref, w2_ref, a0_ref, a1_ref,
         a2_ref, g1_ref, g2_ref, kk_ref, ka_ref, ered_ref, eexp_ref, v0_ref, v1_ref, v2_ref,
         vf_ref, r_o, ld_o, k_o, v_o, na_o, nb_o, g_o) = refs
    else:
        (x_ref, xp_ref, gn_ref, mu_ref, wrkv_ref, w0_ref, w1_ref, w2_ref, a0_ref, a1_ref,
         a2_ref, g1_ref, g2_ref, kk_ref, ka_ref, ered_ref, eexp_ref,
         r_o, ld_o, k_o, v_o, na_o, nb_o, g_o) = refs
    i = pl.program_id(0)
    gn = gn_ref[...]
    h = _rms(x_ref[...], gn)
    hp = _rms(xp_ref[7:8, :], gn) * jnp.where(i > 0, 1.0, 0.0)
    rows = lax.broadcasted_iota(jnp.int32, h.shape, 0)
    hs = jnp.where(rows == 0, hp, pltpu.roll(h, 1, axis=0))
    xx = hs - h
    mu = mu_ref[...]
    xr, xw, xk, xv, xa, xg = [h + xx * mu[j:j + 1] for j in range(6)]
    w_dn = _mm(xw, w1_ref[...])
    a_dn = _mm(xa, a1_ref[...])
    g_dn = _mm(xg, g1_ref[...])
    v_dn = _mm(xv, v1_ref[...]) if has_vres else None
    r = _mm(xr, wrkv_ref[0])
    k = _mm(xk, wrkv_ref[1])
    v = _mm(xv, wrkv_ref[2])
    w = w0_ref[...] + _mm(jnp.tanh(w_dn), w2_ref[...])
    z = -w
    wl = -(jnp.maximum(z, 0.0) + jnp.log(1.0 + jnp.exp(-jnp.abs(z)))) - 0.5
    ld_o[...] = -jnp.exp(wl)
    a = _sigmoid(a0_ref[...] + _mm(a_dn, a2_ref[...]))
    g_o[...] = _mm(_sigmoid(g_dn), g2_ref[...])
    if has_vres:
        v = v + (vf_ref[...].astype(F32) - v) * _sigmoid(v0_ref[...] + _mm(v_dn, v2_ref[...]))
    kkr = k * kk_ref[...]
    norm = jnp.sqrt(_head_sum(kkr * kkr, ered_ref[...]))
    kk = kkr * _spread(1.0 / jnp.maximum(norm, 1e-12), eexp_ref[...])
    r_o[...] = r.astype(BF16)
    k_o[...] = (k * (1.0 + (a - 1.0) * ka_ref[...])).astype(BF16)
    v_o[...] = v.astype(BF16)
    na_o[...] = (-kk).astype(BF16)
    nb_o[...] = (kk * a).astype(BF16)


def _rwkv_proj(x, gn, mu, wrkv, w0, w1, w2, a0, a1, a2, g1, g2, k_k, k_a, e_red, e_exp, vres):
    t, d = x.shape
    tm = TOKEN_TILE
    row = pl.BlockSpec((tm, d), lambda i: (i, 0))
    prev = pl.BlockSpec((8, d), lambda i: (jnp.maximum(i * (tm // 8) - 1, 0), 0))
    args = [x, x, gn, mu, wrkv, w0, w1, w2, a0, a1, a2, g1, g2, k_k, k_a, e_red, e_exp]
    specs = [row, prev, _full(gn.shape), _full(mu.shape), _resident(wrkv.shape), _full(w0.shape),
             _full(w1.shape), _full(w2.shape), _full(a0.shape), _full(a1.shape), _full(a2.shape),
             _full(g1.shape), _full(g2.shape), _full(k_k.shape), _full(k_a.shape),
             _full(e_red.shape), _full(e_exp.shape)]
    if vres is not None:
        v0, v1, v2, v_first = vres
        args += [v0, v1, v2, v_first]
        specs += [_full(v0.shape), _full(v1.shape), _full(v2.shape), row]
    return pl.pallas_call(
        functools.partial(_rwkv_proj_kernel, has_vres=vres is not None),
        grid=(t // tm,),
        in_specs=specs,
        out_specs=[row] * 7,
        out_shape=[jax.ShapeDtypeStruct((t, d), dt) for dt in (BF16, F32, BF16, BF16, BF16, BF16, F32)],
        compiler_params=_params("parallel"),
        name="rwkv_proj",
    )(*args)


def _interleave(primary, filler):
    out = []
    for i in range(max(len(primary), len(filler))):
        if i < len(filler):
            out.append(filler[i])
        if i < len(primary):
            out.append(primary[i])
    return out


def _rwkv_scan_kernel(r_ref, ld_ref, k_ref, v_ref, na_ref, nb_ref, y_ref, s_ref):
    n_pairs = r_ref.shape[1] // LANES
    n_chunks = r_ref.shape[0] // CHUNK
    L = CHUNK
    assert 2 * L == LANES and L == HEAD_DIM and n_chunks % 2 == 0

    @pl.when(pl.program_id(0) == 0)
    def _():
        s_ref[...] = jnp.zeros_like(s_ref)

    ri = lax.broadcasted_iota(jnp.int32, (2 * L, 2 * L), 0)
    ci = lax.broadcasted_iota(jnp.int32, (2 * L, 2 * L), 1)
    same_head = (ri < L) == (ci < L)
    other_head = jnp.logical_not(same_head)
    strict = (ci & (L - 1)) < (ri & (L - 1))
    incl = (ci & (L - 1)) <= (ri & (L - 1))
    eye = (ri == ci).astype(F32)
    head0 = lax.broadcasted_iota(jnp.int32, (L, LANES), 1) < HEAD_DIM
    first_head = lax.broadcasted_iota(jnp.int32, (2 * L, LANES), 1) < HEAD_DIM
    tri = (lax.broadcasted_iota(jnp.int32, (L, L), 0)
           >= lax.broadcasted_iota(jnp.int32, (L, L), 1)).astype(BF16)
    zero = jnp.zeros((2 * L, LANES), BF16)
    pairs = range(n_pairs)
    lanes_of = lambda a, p: a[:, p * LANES:(p + 1) * LANES]
    stack2 = lambda a, b: jnp.concatenate([a, b], axis=0)

    state = [s_ref[p] for p in pairs]

    def prep_stages(chunks):
        items = [(c, p) for c in chunks for p in pairs]
        d = {}

        def elementwise():
            for c in chunks:
                rows = slice(c * L, (c + 1) * L)
                f32 = lambda ref: ref[rows, :].astype(F32)
                r, ld, k = f32(r_ref), ld_ref[rows, :], f32(k_ref)
                v, na, nb = f32(v_ref), f32(na_ref), f32(nb_ref)
                l1 = ld.astype(BF16)
                rem = ld - l1.astype(F32)
                l2 = rem.astype(BF16)
                l3 = (rem - l2.astype(F32)).astype(BF16)
                tri_dot = lambda a: jnp.dot(tri, a, preferred_element_type=F32)
                cum = tri_dot(l1) + (tri_dot(l2) + tri_dot(l3))
                cum_last = cum[L - 1:L, :]
                w_inv = jnp.exp(-cum)
                w_rel = jnp.exp(cum_last - cum)
                d["w_last", c] = jnp.exp(cum_last)
                d["x", c] = stack2(r * jnp.exp(cum), na * jnp.exp(cum - ld)).astype(BF16)
                d["bk", c] = stack2(nb * w_inv, k * w_inv).astype(BF16)
                d["kb", c] = stack2(k * w_inv, nb * w_inv).astype(BF16)
                d["keys", c] = stack2(nb * w_rel, k * w_rel).astype(BF16)
                d["vv", c] = stack2(v, v).astype(BF16)
                d["v", c] = v

        def gram():
            for c, p in items:
                rhs = stack2(jnp.where(first_head, lanes_of(d["bk", c], p), zero),
                             jnp.where(first_head, zero, lanes_of(d["kb", c], p)))
                d["g", c, p] = _mm_nt(lanes_of(d["x", c], p), rhs)

        def split():
            for c, p in items:
                g = d.pop(("g", c, p))
                g0, g1 = g[:, :LANES], g[:, LANES:]
                qs = stack2(g0[L:], g1[L:])
                ms = stack2(g0[:L], g1[:L])
                nil = jnp.where(same_head & strict, qs, 0.0)
                d["m_uv", c, p] = jnp.concatenate(
                    [jnp.where(same_head & incl, ms, 0.0),
                     jnp.where(other_head & incl, ms, 0.0)], axis=1).astype(BF16)
                d["akv", c, p] = _mm(jnp.where(other_head & strict, qs, 0.0),
                                     lanes_of(d["vv", c], p))
                d["t", c, p] = eye + nil
                d["pw", c, p] = _mm(nil, nil)

        def double():
            for c, p in items:
                pw = d["pw", c, p]
                t = d["t", c, p]
                both = _mm(pw, jnp.concatenate([t, pw], axis=1))
                d["t", c, p] = t + both[:, :LANES]
                d["pw", c, p] = both[:, LANES:]

        def finish():
            for c, p in items:
                t = d["t", c, p]
                d["t", c, p] = (t + _mm(d.pop(("pw", c, p)), t)).astype(BF16)

        return [elementwise, gram, split, double, double, double, double, finish], d

    def chunk_stages(c, d):
        loc = {}

        def project():
            for p in pairs:
                loc["xs", p] = _mm_nt(lanes_of(d["x", c], p), state[p])

        def solve():
            for p in pairs:
                xs = loc["xs", p]
                u = _mm(d["t", c, p], stack2(xs[L:], xs[L:]) + d["akv", c, p])
                loc["u", p] = jnp.where(head0, u[:L], u[L:])

        def emit():
            for p in pairs:
                xs, u = loc["xs", p], loc["u", p]
                ub = u.astype(BF16)
                vvp = lanes_of(d["vv", c], p)
                y = stack2(xs[:L], xs[:L]) + _mm(d["m_uv", c, p],
                                                 jnp.concatenate([ub, ub, vvp], axis=0))
                y_ref[c * L:(c + 1) * L, p * LANES:(p + 1) * LANES] = jnp.where(
                    head0, y[:L], y[L:])
                vals = stack2(u, lanes_of(d["v", c], p))
                upd = _mm(vals.T, lanes_of(d["keys", c], p))
                state[p] = jnp.where(
                    same_head, state[p] * lanes_of(d["w_last", c], p) + upd, 0.0)

        return [project, solve, emit]

    groups = [(c, c + 1) for c in range(0, n_chunks, 2)]
    prep, data = prep_stages(groups[0])
    for stage in prep:
        stage()
    for gi, grp in enumerate(groups):
        chain = chunk_stages(grp[0], data) + chunk_stages(grp[1], data)
        if gi + 1 < len(groups):
            prep, nxt = prep_stages(groups[gi + 1])
        else:
            prep, nxt = [], None
        for stage in _interleave(chain, prep):
            stage()
        data = nxt
    for p in pairs:
        s_ref[p] = state[p]


def _rwkv_scan(r, ld, k, v, na, nb):
    t, d = r.shape
    blk = pl.BlockSpec((SCAN_BLOCK, d), lambda c: (c, 0))
    return pl.pallas_call(
        _rwkv_scan_kernel,
        grid=(t // SCAN_BLOCK,),
        in_specs=[blk] * 6,
        out_specs=blk,
        out_shape=jax.ShapeDtypeStruct((t, d), F32),
        scratch_shapes=[pltpu.VMEM((d // LANES, LANES, LANES), F32)],
        compiler_params=_params("arbitrary"),
        name="rwkv_scan",
    )(r, ld, k, v, na, nb)


def _rwkv_post_kernel(y_ref, r_ref, k_ref, v_ref, g_ref, x_ref, lnw_ref, lnb_ref, rk_ref,
                      wo_ref, ered_ref, eexp_ref, o_ref):
    e_red = ered_ref[...]
    e_exp = eexp_ref[...]
    inv_n = 1.0 / HEAD_DIM
    n_sub = 4
    sub = y_ref.shape[0] // n_sub
    blocks = [slice(s * sub, (s + 1) * sub) for s in range(n_sub)]
    ys = [y_ref[b, :] for b in blocks]
    sum_y = [_head_sum(y, e_red) for y in ys]
    sum_b = [_head_sum(r_ref[b, :].astype(F32) * k_ref[b, :].astype(F32) * rk_ref[...], e_red)
             for b in blocks]
    yc = [y - _spread(s * inv_n, e_exp) for y, s in zip(ys, sum_y)]
    bonus = [_spread(s, e_exp) * v_ref[b, :].astype(F32) for s, b in zip(sum_b, blocks)]
    var = [_head_sum(c * c, e_red) * inv_n for c in yc]
    rstd = [_spread(lax.rsqrt(v + GN_EPS), e_exp) for v in var]
    z = [((c * rs * lnw_ref[...] + lnb_ref[...]) + bn) * g_ref[b, :]
         for c, rs, bn, b in zip(yc, rstd, bonus, blocks)]
    o_ref[...] = x_ref[...] + _mm(jnp.concatenate(z, axis=0), wo_ref[...])


def _rwkv_post(y, r, k, v, g, x, ln_w, ln_b, r_k, w_o, e_red, e_exp):
    t, d = x.shape
    tm = TOKEN_TILE
    row = pl.BlockSpec((tm, d), lambda i: (i, 0))
    return pl.pallas_call(
        _rwkv_post_kernel,
        grid=(t // tm,),
        in_specs=[row] * 6 + [_full(ln_w.shape), _full(ln_b.shape), _full(r_k.shape),
                              _resident(w_o.shape), _full(e_red.shape), _full(e_exp.shape)],
        out_specs=row,
        out_shape=jax.ShapeDtypeStruct((t, d), F32),
        compiler_params=_params("parallel"),
        name="rwkv_post",
    )(y, r, k, v, g, x, ln_w, ln_b, r_k, w_o, e_red, e_exp)


def _store_dilated(dst_ref, y, stage_ref, dil):
    rows, d = y.shape
    if dil == 1:
        dst_ref[...] = y.astype(dst_ref.dtype)
        return
    n_tiles = d // LANES
    for c in range(n_tiles):
        stage_ref[c] = y[:, c * LANES:(c + 1) * LANES]
    for r in range(dil):
        for c in range(n_tiles):
            lo = r * d + c * LANES
            dst_ref[:, lo:lo + LANES] = (
                stage_ref[c, pl.ds(r, rows // dil, stride=dil), :].astype(dst_ref.dtype))


def _proj_blocks(hn, w_ref, n_normed, gain_ref, e_red, e_exp, emit):
    d = hn.shape[1]
    n_blk = w_ref.shape[1] // d
    normed = sorted(range(n_normed), reverse=True)
    order = normed + sorted(range(n_normed, n_blk), reverse=True)
    ys, ms = {}, {}

    def project(b):
        ys[b] = jnp.dot(hn, w_ref[:, b * d:(b + 1) * d], preferred_element_type=F32)

    def head_sums(b):
        if b in normed:
            ms[b] = _head_sum(ys[b] * ys[b], e_red) * (1.0 / HEAD_DIM)

    def finish(b):
        y = ys.pop(b)
        if b in normed:
            y = y * _spread(lax.rsqrt(ms.pop(b) + RMS_EPS), e_exp) * gain_ref[b]
        emit(b, y)

    for i in range(n_blk + 2):
        for lag, stage in enumerate((project, head_sums, finish)):
            if 0 <= i - lag < n_blk:
                stage(order[i - lag])


def _kv_proj_kernel(x_ref, gn_ref, w_ref, gain_ref, ered_ref, eexp_ref, o0_ref, o1_ref, o2_ref,
                    *stage_refs):
    n_groups = len(DIL_GROUPS)
    out_refs = (o0_ref, o1_ref, o2_ref)
    hn = _rms(x_ref[...], gn_ref[...]).astype(BF16)

    def emit(b, y):
        is_value, grp = divmod(b, n_groups)
        _store_dilated(out_refs[grp].at[is_value], y, stage_refs[grp], DIL_GROUPS[grp][1])

    _proj_blocks(hn, w_ref, n_groups, gain_ref, ered_ref[...], eexp_ref[...], emit)


def _kv_proj(x, gn, w_kv, gains, e_red, e_exp):
    t, d = x.shape
    tm = TOKEN_TILE
    return pl.pallas_call(
        _kv_proj_kernel,
        grid=(t // tm,),
        in_specs=[pl.BlockSpec((tm, d), lambda i: (i, 0)), _full(gn.shape),
                  _resident(w_kv.shape), _full(gains.shape), _full(e_red.shape),
                  _full(e_exp.shape)],
        out_specs=[pl.BlockSpec((2, tm // dil, dil * d), lambda i: (0, i, 0))
                   for _, dil in DIL_GROUPS],
        out_shape=[jax.ShapeDtypeStruct((2, t // dil, dil * d), BF16) for _, dil in DIL_GROUPS],
        scratch_shapes=[pltpu.VMEM((d // LANES, tm, LANES), F32) for _ in DIL_GROUPS],
        compiler_params=_params("parallel"),
        name="kv_proj",
    )(x, gn, w_kv, gains, e_red, e_exp)


def _q_proj_kernel(x_ref, gn_ref, w_ref, gain_ref, ered_ref, eexp_ref, o0_ref, o1_ref, o2_ref,
                   *stage_refs):
    out_refs = (o0_ref, o1_ref, o2_ref)
    hn = _rms(x_ref[...], gn_ref[...]).astype(BF16)
    scale = HEAD_DIM ** -0.5 * LOG2_E

    def emit(grp, y):
        _store_dilated(out_refs[grp], y * scale, stage_refs[grp], DIL_GROUPS[grp][1])

    _proj_blocks(hn, w_ref, len(DIL_GROUPS), gain_ref, ered_ref[...], eexp_ref[...], emit)


def _q_proj(x, gn, w_q, gains, e_red, e_exp):
    t, d = x.shape
    tm = TOKEN_TILE
    return pl.pallas_call(
        _q_proj_kernel,
        grid=(t // tm,),
        in_specs=[pl.BlockSpec((tm, d), lambda i: (i, 0)), _full(gn.shape),
                  _resident(w_q.shape), _full(gains.shape), _full(e_red.shape),
                  _full(e_exp.shape)],
        out_specs=[pl.BlockSpec((tm // dil, dil * d), lambda i: (i, 0)) for _, dil in DIL_GROUPS],
        out_shape=[jax.ShapeDtypeStruct((t // dil, dil * d), BF16) for _, dil in DIL_GROUPS],
        scratch_shapes=[pltpu.VMEM((d // LANES, tm, LANES), F32) for _ in DIL_GROUPS],
        compiler_params=_params("parallel"),
        name="q_proj",
    )(x, gn, w_q, gains, e_red, e_exp)


def _dil_attn_kernel(q_ref, kp_ref, kc_ref, vp_ref, vc_ref, o_ref, m_ref, l_ref):
    n = pl.program_id(1)
    blk = ATT_BLK
    n_pairs = q_ref.shape[1] // LANES
    n_sub = q_ref.shape[0] // blk
    qi = lax.broadcasted_iota(jnp.int32, (2 * blk, 2 * blk), 0) & (blk - 1)
    kj = lax.broadcasted_iota(jnp.int32, (2 * blk, 2 * blk), 1)
    band = (kj >= qi) & (kj <= qi + blk)
    valid = [band & ((n > 0) | (kj >= blk))] + [band] * (n_sub - 1)
    lane = lax.broadcasted_iota(jnp.int32, (blk, LANES), 1)
    head0 = lane < HEAD_DIM
    dn_nt = (((1,), (1,)), ((), ()))
    zero = jnp.zeros((blk, LANES), BF16)
    lanes_of = lambda a, p: a[:, p * LANES:(p + 1) * LANES]

    def keys_of(prev_ref, cur_ref, s, p):
        if s == 0:
            return jnp.concatenate([lanes_of(prev_ref, p), lanes_of(cur_ref, p)[:blk]], axis=0)
        return lanes_of(cur_ref, p)[(s - 1) * blk:(s + 1) * blk]

    waves = [(s, range(p0, p0 + ATT_WAVE)) for s in range(n_sub)
             for p0 in range(0, n_pairs, ATT_WAVE)]
    scores, probs = {}, {}
    m_acc = [jnp.zeros((blk, LANES), F32) for _ in range(n_sub)]
    l_acc = [jnp.ones((blk, LANES), F32) for _ in range(n_sub)]

    def score_stage(s, ps):
        for p in ps:
            qp = lanes_of(q_ref, p)[s * blk:(s + 1) * blk]
            q2 = jnp.concatenate([jnp.where(head0, qp, zero), jnp.where(head0, zero, qp)], axis=0)
            scores[s, p] = lax.dot_general(q2, keys_of(kp_ref, kc_ref, s, p), dn_nt,
                                           preferred_element_type=F32)

    def softmax_stage(s, ps):
        for p in ps:
            sc = jnp.where(valid[s], scores.pop((s, p)), NEG_INF)
            mx = jnp.max(sc, axis=-1, keepdims=True)
            pe = jnp.exp2(sc - mx)
            ls = jnp.sum(pe, axis=-1, keepdims=True)
            for hh in range(2):
                hit = lane == 2 * p + hh
                rows = slice(hh * blk, (hh + 1) * blk)
                m_acc[s] = jnp.where(hit, mx[rows], m_acc[s])
                l_acc[s] = jnp.where(hit, ls[rows], l_acc[s])
            probs[s, p] = pe.astype(BF16)

    def output_stage(s, ps):
        for p in ps:
            o2 = jnp.dot(probs.pop((s, p)), keys_of(vp_ref, vc_ref, s, p),
                         preferred_element_type=F32)
            o_ref[s * blk:(s + 1) * blk, p * LANES:(p + 1) * LANES] = jnp.where(
                head0, o2[:blk], o2[blk:])

    for i in range(len(waves) + 2):
        for lag, stage in enumerate((score_stage, softmax_stage, output_stage)):
            if 0 <= i - lag < len(waves):
                stage(*waves[i - lag])
    for s in range(n_sub):
        m_ref[s * blk:(s + 1) * blk, :] = m_acc[s]
        l_ref[s * blk:(s + 1) * blk, :] = l_acc[s]


def _dil_attn(q, kv, dil):
    m_rows, wide = q.shape
    d = wide // dil
    blk = ATT_BLK
    step = ATT_SUB * blk
    cur = lambda a: pl.BlockSpec((None, step, d), lambda r, n: (a, n, r))
    prev = lambda a: pl.BlockSpec((None, blk, d),
                                  lambda r, n: (a, jnp.maximum(ATT_SUB * n - 1, 0), r))
    return pl.pallas_call(
        _dil_attn_kernel,
        grid=(dil, m_rows // step),
        in_specs=[pl.BlockSpec((step, d), lambda r, n: (n, r)), prev(0), cur(0), prev(1), cur(1)],
        out_specs=[pl.BlockSpec((step, d), lambda r, n: (n, r)),
                   pl.BlockSpec((step, LANES), lambda r, n: (n, r)),
                   pl.BlockSpec((step, LANES), lambda r, n: (n, r))],
        out_shape=[jax.ShapeDtypeStruct((m_rows, dil * d), F32),
                   jax.ShapeDtypeStruct((m_rows, dil * LANES), F32),
                   jax.ShapeDtypeStruct((m_rows, dil * LANES), F32)],
        compiler_params=_params("parallel", "parallel"),
        name="dil_attn",
    )(q, kv, kv, kv, kv)


def _load_dilated(src_ref, stage_ref, dil):
    if dil == 1:
        return src_ref[...]
    sub = src_ref.shape[0]
    n_tiles = src_ref.shape[1] // dil // LANES
    for r in range(dil):
        for c in range(n_tiles):
            lo = (r * n_tiles + c) * LANES
            stage_ref[c, pl.ds(r, sub, stride=dil), :] = src_ref[:, lo:lo + LANES]
    return jnp.concatenate([stage_ref[c] for c in range(n_tiles)], axis=1)


def _attn_combine_kernel(o0, o1, o2, m0, m1, m2, l0, l1, l2, x_ref, wo_ref, eexp_ref, out_ref,
                         os1, os2, ms1, ms2, ls1, ls2):
    dils = [dil for _, dil in DIL_GROUPS]
    ms = [_load_dilated(ref, st, dil) for ref, st, dil in zip((m0, m1, m2), (None, ms1, ms2), dils)]
    ls = [_load_dilated(ref, st, dil) for ref, st, dil in zip((l0, l1, l2), (None, ls1, ls2), dils)]
    top = jnp.maximum(jnp.maximum(ms[0], ms[1]), ms[2])
    es = [jnp.exp2(mg - top) for mg in ms]
    den = es[0] * ls[0] + es[1] * ls[1] + es[2] * ls[2]
    e_exp = eexp_ref[...]
    acc = None
    for eg, ref, st, dil in zip(es, (o0, o1, o2), (None, os1, os2), dils):
        term = _spread(eg / den, e_exp) * _load_dilated(ref, st, dil)
        acc = term if acc is None else acc + term
    out_ref[...] = x_ref[...] + _mm(acc, wo_ref[...])


def _attn_combine(parts, x, w_o, e_exp):
    t, d = x.shape
    tm = TOKEN_TILE
    row = pl.BlockSpec((tm, d), lambda i: (i, 0))
    dils = [dil for _, dil in DIL_GROUPS]
    o_specs = [pl.BlockSpec((tm // dil, dil * d), lambda i: (i, 0)) for dil in dils]
    s_specs = [pl.BlockSpec((tm // dil, dil * LANES), lambda i: (i, 0)) for dil in dils]
    os_, ms_, ls_ = zip(*parts)
    return pl.pallas_call(
        _attn_combine_kernel,
        grid=(t // tm,),
        in_specs=o_specs + s_specs + s_specs + [row, _resident(w_o.shape), _full(e_exp.shape)],
        out_specs=row,
        out_shape=jax.ShapeDtypeStruct((t, d), F32),
        scratch_shapes=([pltpu.VMEM((d // LANES, tm, LANES), F32)] * 2
                        + [pltpu.VMEM((1, tm, LANES), F32)] * 4),
        compiler_params=_params("parallel"),
        name="attn_combine",
    )(*os_, *ms_, *ls_, x, w_o, e_exp)


def kernel(x, ffn_norm, ffn_w_in, ffn_w_out, mix_norm, rwkv_mu, rwkv_w_rkv, rwkv_w0, rwkv_w1, rwkv_w2, rwkv_a0, rwkv_a1, rwkv_a2, rwkv_v0, rwkv_v1, rwkv_v2, rwkv_g1, rwkv_g2, rwkv_k_k, rwkv_k_a, rwkv_r_k, rwkv_ln_w, rwkv_ln_b, rwkv_w_o, kv_norm, w_kv, k_norm, attn_w_q, q_norm, attn_w_o):
    bsz, t, d = x.shape
    assert bsz == 1 and d % LANES == 0 and t % max(w for w, _ in DIL_GROUPS) == 0
    assert all(w // dil == ATT_BLK for w, dil in DIL_GROUPS)
    depth = ffn_norm.shape[0]
    n_a = rwkv_mu.shape[0]
    n_groups = len(DIL_GROUPS)
    row = lambda p: p.reshape(1, -1)

    ffn_g = ffn_norm.reshape(2 * depth, 1, d)
    ffn_wi = ffn_w_in.astype(BF16).reshape(2 * depth, d, -1)
    ffn_wo = ffn_w_out.reshape(2 * depth, -1, d)
    head_of_lane = jnp.arange(d) // HEAD_DIM
    one_hot = (jnp.arange(LANES)[:, None] == head_of_lane[None, :]).astype(BF16)
    e_red = one_hot.T
    e_exp = jnp.concatenate([one_hot, one_hot], axis=0)

    xs = x.reshape(t, d)
    v_first = None
    kvs = None
    for l in range(depth):
        xs = _ffn(xs, ffn_g, ffn_wi, ffn_wo, 2 * l)
        if l < n_a:
            vres = None
            if l > 0:
                vres = (row(rwkv_v0[l - 1]), rwkv_v1[l - 1].astype(BF16),
                        rwkv_v2[l - 1].astype(BF16), v_first)
            r, ld, k, v, na, nb, g = _rwkv_proj(
                xs, row(mix_norm[l]), rwkv_mu[l], rwkv_w_rkv[l].astype(BF16), row(rwkv_w0[l]),
                rwkv_w1[l].astype(BF16), rwkv_w2[l].astype(BF16), row(rwkv_a0[l]),
                rwkv_a1[l].astype(BF16), rwkv_a2[l].astype(BF16), rwkv_g1[l].astype(BF16),
                rwkv_g2[l].astype(BF16), row(rwkv_k_k[l]), row(rwkv_k_a[l]), e_red, e_exp, vres)
            if l == 0:
                v_first = v
            y = _rwkv_scan(r, ld, k, v, na, nb)
            xs = _rwkv_post(y, r, k, v, g, xs, row(rwkv_ln_w[l]), row(rwkv_ln_b[l]),
                            row(rwkv_r_k[l]), rwkv_w_o[l].astype(BF16), e_red, e_exp)
        else:
            i = l - n_a
            q_gain = jnp.tile(q_norm[i], (1, d // HEAD_DIM)).reshape(n_groups, 1, d)
            qs = _q_proj(xs, row(mix_norm[l]), attn_w_q[i].astype(BF16), q_gain, e_red, e_exp)
            parts = [_dil_attn(qs[grp], kvs[grp], DIL_GROUPS[grp][1]) for grp in range(n_groups)]
            xs = _attn_combine(parts, xs, attn_w_o[i].astype(BF16), e_exp)
        xs = _ffn(xs, ffn_g, ffn_wi, ffn_wo, 2 * l + 1)
        if l == n_a - 1:
            k_gain = jnp.tile(k_norm, (1, d // HEAD_DIM)).reshape(n_groups, 1, d)
            kvs = _kv_proj(xs, row(kv_norm), w_kv.astype(BF16), k_gain, e_red, e_exp)
    return xs.reshape(bsz, t, d)
```

```python
import functools

import jax
import jax.numpy as jnp
from jax import lax
from jax.experimental import pallas as pl
from jax.experimental.pallas import tpu as pltpu

F32 = jnp.float32
BF16 = jnp.bfloat16

RMS_EPS = 1e-6
GN_EPS = 64e-5
HEAD_DIM = 64
LANES = 128
DIL_GROUPS = ((128, 1), (512, 4), (2048, 16))
ATT_BLK = 128
ATT_SUB = 4
ATT_WAVE = 1
NEG_INF = -1e30
LOG2_E = 1.4426950408889634
CHUNK = 64
SCAN_BLOCK = 512
TOKEN_TILE = 512
FFN_COL_TILE = 256
VMEM_LIMIT = 56 * 1024 * 1024


def _params(*sem):
    return pltpu.CompilerParams(dimension_semantics=sem, vmem_limit_bytes=VMEM_LIMIT)


def _mm(a, b):
    return jnp.dot(a.astype(BF16), b.astype(BF16), preferred_element_type=F32)


def _mm_nt(a, b):
    return lax.dot_general(a.astype(BF16), b.astype(BF16), (((1,), (1,)), ((), ())),
                           preferred_element_type=F32)


def _rms(x, g):
    return x * lax.rsqrt(jnp.mean(x * x, axis=-1, keepdims=True) + RMS_EPS) * g


def _sigmoid(x):
    return 1.0 / (1.0 + jnp.exp(-x))


def _full(shape):
    n = len(shape)
    return pl.BlockSpec(shape, lambda *_: (0,) * n)


def _resident(shape):
    n = len(shape)
    return pl.BlockSpec(shape, lambda *_: (0,) * n, pipeline_mode=pl.Buffered(1))


def _ffn_kernel(x_ref, g_ref, win_ref, wout_ref, o_ref, act_ref):
    x = x_ref[...]
    hn = _rms(x, g_ref[...]).astype(BF16)
    d_ff = wout_ref.shape[0]
    for c in range(d_ff // FFN_COL_TILE):
        lo = c * FFN_COL_TILE
        gate = jnp.dot(hn, win_ref[:, lo:lo + FFN_COL_TILE], preferred_element_type=F32)
        up = jnp.dot(hn, win_ref[:, d_ff + lo:d_ff + lo + FFN_COL_TILE],
                     preferred_element_type=F32)
        act_ref[:, lo:lo + FFN_COL_TILE] = (gate * _sigmoid(gate) * up).astype(BF16)
    o_ref[...] = x + 0.5 * jnp.dot(act_ref[...], wout_ref[...].astype(BF16),
                                   preferred_element_type=F32)


def _ffn(x, norms, w_in, w_out, idx):
    t, d = x.shape
    d_ff = w_out.shape[1]
    tm = TOKEN_TILE
    return pl.pallas_call(
        _ffn_kernel,
        grid=(t // tm,),
        in_specs=[
            pl.BlockSpec((tm, d), lambda i: (i, 0)),
            pl.BlockSpec((None, 1, d), lambda i: (idx, 0, 0)),
            pl.BlockSpec((None, d, 2 * d_ff), lambda i: (idx, 0, 0), pipeline_mode=pl.Buffered(1)),
            pl.BlockSpec((None, d_ff, d), lambda i: (idx, 0, 0), pipeline_mode=pl.Buffered(1)),
        ],
        out_specs=pl.BlockSpec((tm, d), lambda i: (i, 0)),
        out_shape=jax.ShapeDtypeStruct((t, d), F32),
        scratch_shapes=[pltpu.VMEM((tm, d_ff), BF16)],
        compiler_params=_params("parallel"),
        name="ffn",
    )(x, norms, w_in, w_out)


def _head_sum(x, e_red):
    return jnp.dot(x.astype(BF16), e_red, preferred_element_type=F32)


def _spread(c, e_exp):
    hi = c.astype(BF16)
    lo = (c - hi.astype(F32)).astype(BF16)
    return jnp.dot(jnp.concatenate([hi, lo], axis=1), e_exp, preferred_element_type=F32)


def _rwkv_proj_kernel(*refs, has_vres):
    if has_vres:
        (x_ref, xp_ref, gn_ref, mu_ref, wrkv_ref, w0_ref, w1_ref, w2_ref, a0_ref, a1_ref,
         a2_ref, g1_ref, g2_ref, kk_ref, ka_ref, ered_ref, eexp_ref, v0_ref, v1_ref, v2_ref,
         vf_ref, r_o, ld_o, k_o, v_o, na_o, nb_o, g_o) = refs
    else:
        (x_ref, xp_ref, gn_ref, mu_ref, wrkv_ref, w0_ref, w1_ref, w2_ref, a0_ref, a1_ref,
         a2_ref, g1_ref, g2_ref, kk_ref, ka_ref, ered_ref, eexp_ref,
         r_o, ld_o, k_o, v_o, na_o, nb_o, g_o) = refs
    i = pl.program_id(0)
    gn = gn_ref[...]
    h = _rms(x_ref[...], gn)
    hp = _rms(xp_ref[7:8, :], gn) * jnp.where(i > 0, 1.0, 0.0)
    rows = lax.broadcasted_iota(jnp.int32, h.shape, 0)
    hs = jnp.where(rows == 0, hp, pltpu.roll(h, 1, axis=0))
    xx = hs - h
    mu = mu_ref[...]
    xr, xw, xk, xv, xa, xg = [h + xx * mu[j:j + 1] for j in range(6)]
    w_dn = _mm(xw, w1_ref[...])
    a_dn = _mm(xa, a1_ref[...])
    g_dn = _mm(xg, g1_ref[...])
    v_dn = _mm(xv, v1_ref[...]) if has_vres else None
    r = _mm(xr, wrkv_ref[0])
    k = _mm(xk, wrkv_ref[1])
    v = _mm(xv, wrkv_ref[2])
    w = w0_ref[...] + _mm(jnp.tanh(w_dn), w2_ref[...])
    z = -w
    wl = -(jnp.maximum(z, 0.0) + jnp.log(1.0 + jnp.exp(-jnp.abs(z)))) - 0.5
    ld_o[...] = -jnp.exp(wl)
    a = _sigmoid(a0_ref[...] + _mm(a_dn, a2_ref[...]))
    g_o[...] = _mm(_sigmoid(g_dn), g2_ref[...]).astype(BF16)
    if has_vres:
        v = v + (vf_ref[...].astype(F32) - v) * _sigmoid(v0_ref[...] + _mm(v_dn, v2_ref[...]))
    kkr = k * kk_ref[...]
    norm = jnp.sqrt(_head_sum(kkr * kkr, ered_ref[...]))
    kk = kkr * _spread(1.0 / jnp.maximum(norm, 1e-12), eexp_ref[...])
    r_o[...] = r.astype(BF16)
    k_o[...] = (k * (1.0 + (a - 1.0) * ka_ref[...])).astype(BF16)
    v_o[...] = v.astype(BF16)
    na_o[...] = (-kk).astype(BF16)
    nb_o[...] = (kk * a).astype(BF16)


def _rwkv_proj(x, gn, mu, wrkv, w0, w1, w2, a0, a1, a2, g1, g2, k_k, k_a, e_red, e_exp, vres):
    t, d = x.shape
    tm = TOKEN_TILE
    row = pl.BlockSpec((tm, d), lambda i: (i, 0))
    prev = pl.BlockSpec((8, d), lambda i: (jnp.maximum(i * (tm // 8) - 1, 0), 0))
    args = [x, x, gn, mu, wrkv, w0, w1, w2, a0, a1, a2, g1, g2, k_k, k_a, e_red, e_exp]
    specs = [row, prev, _full(gn.shape), _full(mu.shape), _resident(wrkv.shape), _full(w0.shape),
             _full(w1.shape), _full(w2.shape), _full(a0.shape), _full(a1.shape), _full(a2.shape),
             _full(g1.shape), _full(g2.shape), _full(k_k.shape), _full(k_a.shape),
             _full(e_red.shape), _full(e_exp.shape)]
    if vres is not None:
        v0, v1, v2, v_first = vres
        args += [v0, v1, v2, v_first]
        specs += [_full(v0.shape), _full(v1.shape), _full(v2.shape), row]
    return pl.pallas_call(
        functools.partial(_rwkv_proj_kernel, has_vres=vres is not None),
        grid=(t // tm,),
        in_specs=specs,
        out_specs=[row] * 7,
        out_shape=[jax.ShapeDtypeStruct((t, d), dt) for dt in (BF16, F32, BF16, BF16, BF16, BF16, BF16)],
        compiler_params=_params("parallel"),
        name="rwkv_proj",
    )(*args)


def _interleave(primary, filler):
    out = []
    for i in range(max(len(primary), len(filler))):
        if i < len(filler):
            out.append(filler[i])
        if i < len(primary):
            out.append(primary[i])
    return out


def _rwkv_scan_kernel(r_ref, ld_ref, k_ref, v_ref, na_ref, nb_ref, y_ref, s_ref):
    n_pairs = r_ref.shape[1] // LANES
    n_chunks = r_ref.shape[0] // CHUNK
    L = CHUNK
    assert 2 * L == LANES and L == HEAD_DIM and n_chunks % 2 == 0

    @pl.when(pl.program_id(0) == 0)
    def _():
        s_ref[...] = jnp.zeros_like(s_ref)

    ri = lax.broadcasted_iota(jnp.int32, (2 * L, 2 * L), 0)
    ci = lax.broadcasted_iota(jnp.int32, (2 * L, 2 * L), 1)
    same_head = (ri < L) == (ci < L)
    other_head = jnp.logical_not(same_head)
    strict = (ci & (L - 1)) < (ri & (L - 1))
    incl = (ci & (L - 1)) <= (ri & (L - 1))
    eye = (ri == ci).astype(F32)
    head0 = lax.broadcasted_iota(jnp.int32, (L, LANES), 1) < HEAD_DIM
    first_head = lax.broadcasted_iota(jnp.int32, (2 * L, LANES), 1) < HEAD_DIM
    tri = (lax.broadcasted_iota(jnp.int32, (L, L), 0)
           >= lax.broadcasted_iota(jnp.int32, (L, L), 1)).astype(BF16)
    zero = jnp.zeros((2 * L, LANES), BF16)
    pairs = range(n_pairs)
    lanes_of = lambda a, p: a[:, p * LANES:(p + 1) * LANES]
    stack2 = lambda a, b: jnp.concatenate([a, b], axis=0)

    state = [s_ref[p] for p in pairs]

    def prep_stages(chunks):
        items = [(c, p) for c in chunks for p in pairs]
        d = {}

        def elementwise():
            for c in chunks:
                rows = slice(c * L, (c + 1) * L)
                f32 = lambda ref: ref[rows, :].astype(F32)
                r, ld, k = f32(r_ref), ld_ref[rows, :], f32(k_ref)
                v, na, nb = f32(v_ref), f32(na_ref), f32(nb_ref)
                l1 = ld.astype(BF16)
                rem = ld - l1.astype(F32)
                l2 = rem.astype(BF16)
                l3 = (rem - l2.astype(F32)).astype(BF16)
                tri_dot = lambda a: jnp.dot(tri, a, preferred_element_type=F32)
                cum = tri_dot(l1) + (tri_dot(l2) + tri_dot(l3))
                cum_last = cum[L - 1:L, :]
                w_inv = jnp.exp(-cum)
                w_rel = jnp.exp(cum_last - cum)
                d["w_last", c] = jnp.exp(cum_last)
                d["x", c] = stack2(r * jnp.exp(cum), na * jnp.exp(cum - ld)).astype(BF16)
                d["bk", c] = stack2(nb * w_inv, k * w_inv).astype(BF16)
                d["kb", c] = stack2(k * w_inv, nb * w_inv).astype(BF16)
                d["keys", c] = stack2(nb * w_rel, k * w_rel).astype(BF16)
                d["vv", c] = stack2(v, v).astype(BF16)
                d["v", c] = v

        def gram():
            for c, p in items:
                rhs = stack2(jnp.where(first_head, lanes_of(d["bk", c], p), zero),
                             jnp.where(first_head, zero, lanes_of(d["kb", c], p)))
                d["g", c, p] = _mm_nt(lanes_of(d["x", c], p), rhs)

        def split():
            for c, p in items:
                g = d.pop(("g", c, p))
                g0, g1 = g[:, :LANES], g[:, LANES:]
                qs = stack2(g0[L:], g1[L:])
                ms = stack2(g0[:L], g1[:L])
                nil = jnp.where(same_head & strict, qs, 0.0)
                d["m_uv", c, p] = jnp.concatenate(
                    [jnp.where(same_head & incl, ms, 0.0),
                     jnp.where(other_head & incl, ms, 0.0)], axis=1).astype(BF16)
                d["akv", c, p] = _mm(jnp.where(other_head & strict, qs, 0.0),
                                     lanes_of(d["vv", c], p))
                d["t", c, p] = eye + nil
                d["pw", c, p] = _mm(nil, nil)

        def double():
            for c, p in items:
                pw = d["pw", c, p]
                t = d["t", c, p]
                both = _mm(pw, jnp.concatenate([t, pw], axis=1))
                d["t", c, p] = t + both[:, :LANES]
                d["pw", c, p] = both[:, LANES:]

        def finish():
            for c, p in items:
                t = d["t", c, p]
                d["t", c, p] = (t + _mm(d.pop(("pw", c, p)), t)).astype(BF16)

        return [elementwise, gram, split, double, double, double, double, finish], d

    def chunk_stages(c, d):
        loc = {}

        def project():
            for p in pairs:
                loc["xs", p] = _mm_nt(lanes_of(d["x", c], p), state[p])

        def solve():
            for p in pairs:
                xs = loc["xs", p]
                u = _mm(d["t", c, p], stack2(xs[L:], xs[L:]) + d["akv", c, p])
                loc["u", p] = jnp.where(head0, u[:L], u[L:])

        def emit():
            for p in pairs:
                xs, u = loc["xs", p], loc["u", p]
                ub = u.astype(BF16)
                vvp = lanes_of(d["vv", c], p)
                y = stack2(xs[:L], xs[:L]) + _mm(d["m_uv", c, p],
                                                 jnp.concatenate([ub, ub, vvp], axis=0))
                y_ref[c * L:(c + 1) * L, p * LANES:(p + 1) * LANES] = jnp.where(
                    head0, y[:L], y[L:]).astype(y_ref.dtype)
                vals = stack2(u, lanes_of(d["v", c], p))
                upd = _mm(vals.T, lanes_of(d["keys", c], p))
                state[p] = jnp.where(
                    same_head, state[p] * lanes_of(d["w_last", c], p) + upd, 0.0)

        return [project, solve, emit]

    groups = [(c, c + 1) for c in range(0, n_chunks, 2)]
    prep, data = prep_stages(groups[0])
    for stage in prep:
        stage()
    for gi, grp in enumerate(groups):
        chain = chunk_stages(grp[0], data) + chunk_stages(grp[1], data)
        if gi + 1 < len(groups):
            prep, nxt = prep_stages(groups[gi + 1])
        else:
            prep, nxt = [], None
        for stage in _interleave(chain, prep):
            stage()
        data = nxt
    for p in pairs:
        s_ref[p] = state[p]


def _rwkv_scan(r, ld, k, v, na, nb):
    t, d = r.shape
    blk = pl.BlockSpec((SCAN_BLOCK, d), lambda c: (c, 0))
    return pl.pallas_call(
        _rwkv_scan_kernel,
        grid=(t // SCAN_BLOCK,),
        in_specs=[blk] * 6,
        out_specs=blk,
        out_shape=jax.ShapeDtypeStruct((t, d), BF16),
        scratch_shapes=[pltpu.VMEM((d // LANES, LANES, LANES), F32)],
        compiler_params=_params("arbitrary"),
        name="rwkv_scan",
    )(r, ld, k, v, na, nb)


def _rwkv_post_kernel(y_ref, r_ref, k_ref, v_ref, g_ref, x_ref, lnw_ref, lnb_ref, rk_ref,
                      wo_ref, ered_ref, eexp_ref, o_ref):
    e_red = ered_ref[...]
    e_exp = eexp_ref[...]
    inv_n = 1.0 / HEAD_DIM
    n_sub = 4
    sub = y_ref.shape[0] // n_sub
    blocks = [slice(s * sub, (s + 1) * sub) for s in range(n_sub)]
    ys = [y_ref[b, :].astype(F32) for b in blocks]
    sum_y = [_head_sum(y, e_red) for y in ys]
    sum_b = [_head_sum(r_ref[b, :].astype(F32) * k_ref[b, :].astype(F32) * rk_ref[...], e_red)
             for b in blocks]
    yc = [y - _spread(s * inv_n, e_exp) for y, s in zip(ys, sum_y)]
    bonus = [_spread(s, e_exp) * v_ref[b, :].astype(F32) for s, b in zip(sum_b, blocks)]
    var = [_head_sum(c * c, e_red) * inv_n for c in yc]
    rstd = [_spread(lax.rsqrt(v + GN_EPS), e_exp) for v in var]
    z = [((c * rs * lnw_ref[...] + lnb_ref[...]) + bn) * g_ref[b, :].astype(F32)
         for c, rs, bn, b in zip(yc, rstd, bonus, blocks)]
    o_ref[...] = x_ref[...] + _mm(jnp.concatenate(z, axis=0), wo_ref[...])


def _rwkv_post(y, r, k, v, g, x, ln_w, ln_b, r_k, w_o, e_red, e_exp):
    t, d = x.shape
    tm = TOKEN_TILE
    row = pl.BlockSpec((tm, d), lambda i: (i, 0))
    return pl.pallas_call(
        _rwkv_post_kernel,
        grid=(t // tm,),
        in_specs=[row] * 6 + [_full(ln_w.shape), _full(ln_b.shape), _full(r_k.shape),
                              _resident(w_o.shape), _full(e_red.shape), _full(e_exp.shape)],
        out_specs=row,
        out_shape=jax.ShapeDtypeStruct((t, d), F32),
        compiler_params=_params("parallel"),
        name="rwkv_post",
    )(y, r, k, v, g, x, ln_w, ln_b, r_k, w_o, e_red, e_exp)


def _store_dilated(dst_ref, y, stage_ref, dil):
    rows, d = y.shape
    if dil == 1:
        dst_ref[...] = y.astype(dst_ref.dtype)
        return
    n_tiles = d // LANES
    for c in range(n_tiles):
        stage_ref[c] = y[:, c * LANES:(c + 1) * LANES]
    for r in range(dil):
        for c in range(n_tiles):
            lo = r * d + c * LANES
            dst_ref[:, lo:lo + LANES] = (
                stage_ref[c, pl.ds(r, rows // dil, stride=dil), :].astype(dst_ref.dtype))


def _proj_blocks(hn, w_ref, n_normed, gain_ref, e_red, e_exp, emit):
    d = hn.shape[1]
    n_blk = w_ref.shape[1] // d
    normed = sorted(range(n_normed), reverse=True)
    order = normed + sorted(range(n_normed, n_blk), reverse=True)
    ys, ms = {}, {}

    def project(b):
        ys[b] = jnp.dot(hn, w_ref[:, b * d:(b + 1) * d], preferred_element_type=F32)

    def head_sums(b):
        if b in normed:
            ms[b] = _head_sum(ys[b] * ys[b], e_red) * (1.0 / HEAD_DIM)

    def finish(b):
        y = ys.pop(b)
        if b in normed:
            y = y * _spread(lax.rsqrt(ms.pop(b) + RMS_EPS), e_exp) * gain_ref[b]
        emit(b, y)

    for i in range(n_blk + 2):
        for lag, stage in enumerate((project, head_sums, finish)):
            if 0 <= i - lag < n_blk:
                stage(order[i - lag])


def _kv_proj_kernel(x_ref, gn_ref, w_ref, gain_ref, ered_ref, eexp_ref, o0_ref, o1_ref, o2_ref,
                    *stage_refs):
    n_groups = len(DIL_GROUPS)
    out_refs = (o0_ref, o1_ref, o2_ref)
    hn = _rms(x_ref[...], gn_ref[...]).astype(BF16)

    def emit(b, y):
        is_value, grp = divmod(b, n_groups)
        _store_dilated(out_refs[grp].at[is_value], y, stage_refs[grp], DIL_GROUPS[grp][1])

    _proj_blocks(hn, w_ref, n_groups, gain_ref, ered_ref[...], eexp_ref[...], emit)


def _kv_proj(x, gn, w_kv, gains, e_red, e_exp):
    t, d = x.shape
    tm = TOKEN_TILE
    return pl.pallas_call(
        _kv_proj_kernel,
        grid=(t // tm,),
        in_specs=[pl.BlockSpec((tm, d), lambda i: (i, 0)), _full(gn.shape),
                  _resident(w_kv.shape), _full(gains.shape), _full(e_red.shape),
                  _full(e_exp.shape)],
        out_specs=[pl.BlockSpec((2, tm // dil, dil * d), lambda i: (0, i, 0))
                   for _, dil in DIL_GROUPS],
        out_shape=[jax.ShapeDtypeStruct((2, t // dil, dil * d), BF16) for _, dil in DIL_GROUPS],
        scratch_shapes=[pltpu.VMEM((d // LANES, tm, LANES), F32) for _ in DIL_GROUPS],
        compiler_params=_params("parallel"),
        name="kv_proj",
    )(x, gn, w_kv, gains, e_red, e_exp)


def _q_proj_kernel(x_ref, gn_ref, w_ref, gain_ref, ered_ref, eexp_ref, o0_ref, o1_ref, o2_ref,
                   *stage_refs):
    out_refs = (o0_ref, o1_ref, o2_ref)
    hn = _rms(x_ref[...], gn_ref[...]).astype(BF16)
    scale = HEAD_DIM ** -0.5 * LOG2_E

    def emit(grp, y):
        _store_dilated(out_refs[grp], y * scale, stage_refs[grp], DIL_GROUPS[grp][1])

    _proj_blocks(hn, w_ref, len(DIL_GROUPS), gain_ref, ered_ref[...], eexp_ref[...], emit)


def _q_proj(x, gn, w_q, gains, e_red, e_exp):
    t, d = x.shape
    tm = TOKEN_TILE
    return pl.pallas_call(
        _q_proj_kernel,
        grid=(t // tm,),
        in_specs=[pl.BlockSpec((tm, d), lambda i: (i, 0)), _full(gn.shape),
                  _resident(w_q.shape), _full(gains.shape), _full(e_red.shape),
                  _full(e_exp.shape)],
        out_specs=[pl.BlockSpec((tm // dil, dil * d), lambda i: (i, 0)) for _, dil in DIL_GROUPS],
        out_shape=[jax.ShapeDtypeStruct((t // dil, dil * d), BF16) for _, dil in DIL_GROUPS],
        scratch_shapes=[pltpu.VMEM((d // LANES, tm, LANES), F32) for _ in DIL_GROUPS],
        compiler_params=_params("parallel"),
        name="q_proj",
    )(x, gn, w_q, gains, e_red, e_exp)


def _dil_attn_kernel(q_ref, kp_ref, kc_ref, vp_ref, vc_ref, o_ref, m_ref, l_ref):
    n = pl.program_id(1)
    blk = ATT_BLK
    n_pairs = q_ref.shape[1] // LANES
    n_sub = q_ref.shape[0] // blk
    qi = lax.broadcasted_iota(jnp.int32, (2 * blk, 2 * blk), 0) & (blk - 1)
    kj = lax.broadcasted_iota(jnp.int32, (2 * blk, 2 * blk), 1)
    band = (kj >= qi) & (kj <= qi + blk)
    valid = [band & ((n > 0) | (kj >= blk))] + [band] * (n_sub - 1)
    lane = lax.broadcasted_iota(jnp.int32, (blk, LANES), 1)
    head0 = lane < HEAD_DIM
    dn_nt = (((1,), (1,)), ((), ()))
    zero = jnp.zeros((blk, LANES), BF16)
    lanes_of = lambda a, p: a[:, p * LANES:(p + 1) * LANES]

    def keys_of(prev_ref, cur_ref, s, p):
        if s == 0:
            return jnp.concatenate([lanes_of(prev_ref, p), lanes_of(cur_ref, p)[:blk]], axis=0)
        return lanes_of(cur_ref, p)[(s - 1) * blk:(s + 1) * blk]

    waves = [(s, range(p0, p0 + ATT_WAVE)) for s in range(n_sub)
             for p0 in range(0, n_pairs, ATT_WAVE)]
    scores, probs = {}, {}
    m_acc = [jnp.zeros((blk, LANES), F32) for _ in range(n_sub)]
    l_acc = [jnp.ones((blk, LANES), F32) for _ in range(n_sub)]

    def score_stage(s, ps):
        for p in ps:
            qp = lanes_of(q_ref, p)[s * blk:(s + 1) * blk]
            q2 = jnp.concatenate([jnp.where(head0, qp, zero), jnp.where(head0, zero, qp)], axis=0)
            scores[s, p] = lax.dot_general(q2, keys_of(kp_ref, kc_ref, s, p), dn_nt,
                                           preferred_element_type=F32)

    def softmax_stage(s, ps):
        for p in ps:
            sc = jnp.where(valid[s], scores.pop((s, p)), NEG_INF)
            mx = jnp.max(sc, axis=-1, keepdims=True)
            pe = jnp.exp2(sc - mx)
            ls = jnp.sum(pe, axis=-1, keepdims=True)
            for hh in range(2):
                hit = lane == 2 * p + hh
                rows = slice(hh * blk, (hh + 1) * blk)
                m_acc[s] = jnp.where(hit, mx[rows], m_acc[s])
                l_acc[s] = jnp.where(hit, ls[rows], l_acc[s])
            probs[s, p] = pe.astype(BF16)

    def output_stage(s, ps):
        for p in ps:
            o2 = jnp.dot(probs.pop((s, p)), keys_of(vp_ref, vc_ref, s, p),
                         preferred_element_type=F32)
            o_ref[s * blk:(s + 1) * blk, p * LANES:(p + 1) * LANES] = jnp.where(
                head0, o2[:blk], o2[blk:]).astype(o_ref.dtype)

    for i in range(len(waves) + 2):
        for lag, stage in enumerate((score_stage, softmax_stage, output_stage)):
            if 0 <= i - lag < len(waves):
                stage(*waves[i - lag])
    for s in range(n_sub):
        m_ref[s * blk:(s + 1) * blk, :] = m_acc[s]
        l_ref[s * blk:(s + 1) * blk, :] = l_acc[s]


def _dil_attn(q, kv, dil):
    m_rows, wide = q.shape
    d = wide // dil
    blk = ATT_BLK
    step = ATT_SUB * blk
    cur = lambda a: pl.BlockSpec((None, step, d), lambda r, n: (a, n, r))
    prev = lambda a: pl.BlockSpec((None, blk, d),
                                  lambda r, n: (a, jnp.maximum(ATT_SUB * n - 1, 0), r))
    return pl.pallas_call(
        _dil_attn_kernel,
        grid=(dil, m_rows // step),
        in_specs=[pl.BlockSpec((step, d), lambda r, n: (n, r)), prev(0), cur(0), prev(1), cur(1)],
        out_specs=[pl.BlockSpec((step, d), lambda r, n: (n, r)),
                   pl.BlockSpec((step, LANES), lambda r, n: (n, r)),
                   pl.BlockSpec((step, LANES), lambda r, n: (n, r))],
        out_shape=[jax.ShapeDtypeStruct((m_rows, dil * d), BF16),
                   jax.ShapeDtypeStruct((m_rows, dil * LANES), F32),
                   jax.ShapeDtypeStruct((m_rows, dil * LANES), F32)],
        compiler_params=_params("parallel", "parallel"),
        name="dil_attn",
    )(q, kv, kv, kv, kv)


def _load_dilated(src_ref, stage_ref, dil):
    if dil == 1:
        return src_ref[...]
    sub = src_ref.shape[0]
    n_tiles = src_ref.shape[1] // dil // LANES
    for r in range(dil):
        for c in range(n_tiles):
            lo = (r * n_tiles + c) * LANES
            stage_ref[c, pl.ds(r, sub, stride=dil), :] = src_ref[:, lo:lo + LANES].astype(F32)
    return jnp.concatenate([stage_ref[c] for c in range(n_tiles)], axis=1)


def _attn_combine_kernel(o0, o1, o2, m0, m1, m2, l0, l1, l2, x_ref, wo_ref, eexp_ref, out_ref,
                         os1, os2, ms1, ms2, ls1, ls2):
    dils = [dil for _, dil in DIL_GROUPS]
    ms = [_load_dilated(ref, st, dil) for ref, st, dil in zip((m0, m1, m2), (None, ms1, ms2), dils)]
    ls = [_load_dilated(ref, st, dil) for ref, st, dil in zip((l0, l1, l2), (None, ls1, ls2), dils)]
    top = jnp.maximum(jnp.maximum(ms[0], ms[1]), ms[2])
    es = [jnp.exp2(mg - top) for mg in ms]
    den = es[0] * ls[0] + es[1] * ls[1] + es[2] * ls[2]
    e_exp = eexp_ref[...]
    acc = None
    for eg, ref, st, dil in zip(es, (o0, o1, o2), (None, os1, os2), dils):
        term = _spread(eg / den, e_exp) * _load_dilated(ref, st, dil)
        acc = term if acc is None else acc + term
    out_ref[...] = x_ref[...] + _mm(acc, wo_ref[...])


def _attn_combine(parts, x, w_o, e_exp):
    t, d = x.shape
    tm = TOKEN_TILE
    row = pl.BlockSpec((tm, d), lambda i: (i, 0))
    dils = [dil for _, dil in DIL_GROUPS]
    o_specs = [pl.BlockSpec((tm // dil, dil * d), lambda i: (i, 0)) for dil in dils]
    s_specs = [pl.BlockSpec((tm // dil, dil * LANES), lambda i: (i, 0)) for dil in dils]
    os_, ms_, ls_ = zip(*parts)
    return pl.pallas_call(
        _attn_combine_kernel,
        grid=(t // tm,),
        in_specs=o_specs + s_specs + s_specs + [row, _resident(w_o.shape), _full(e_exp.shape)],
        out_specs=row,
        out_shape=jax.ShapeDtypeStruct((t, d), F32),
        scratch_shapes=([pltpu.VMEM((d // LANES, tm, LANES), F32)] * 2
                        + [pltpu.VMEM((1, tm, LANES), F32)] * 4),
        compiler_params=_params("parallel"),
        name="attn_combine",
    )(*os_, *ms_, *ls_, x, w_o, e_exp)


def kernel(x, ffn_norm, ffn_w_in, ffn_w_out, mix_norm, rwkv_mu, rwkv_w_rkv, rwkv_w0, rwkv_w1, rwkv_w2, rwkv_a0, rwkv_a1, rwkv_a2, rwkv_v0, rwkv_v1, rwkv_v2, rwkv_g1, rwkv_g2, rwkv_k_k, rwkv_k_a, rwkv_r_k, rwkv_ln_w, rwkv_ln_b, rwkv_w_o, kv_norm, w_kv, k_norm, attn_w_q, q_norm, attn_w_o):
    bsz, t, d = x.shape
    assert bsz == 1 and d % LANES == 0 and t % max(w for w, _ in DIL_GROUPS) == 0
    assert all(w // dil == ATT_BLK for w, dil in DIL_GROUPS)
    depth = ffn_norm.shape[0]
    n_a = rwkv_mu.shape[0]
    n_groups = len(DIL_GROUPS)
    row = lambda p: p.reshape(1, -1)

    ffn_g = ffn_norm.reshape(2 * depth, 1, d)
    ffn_wi = ffn_w_in.astype(BF16).reshape(2 * depth, d, -1)
    ffn_wo = ffn_w_out.reshape(2 * depth, -1, d)
    head_of_lane = jnp.arange(d) // HEAD_DIM
    one_hot = (jnp.arange(LANES)[:, None] == head_of_lane[None, :]).astype(BF16)
    e_red = one_hot.T
    e_exp = jnp.concatenate([one_hot, one_hot], axis=0)

    xs = x.reshape(t, d)
    v_first = None
    kvs = None
    for l in range(depth):
        xs = _ffn(xs, ffn_g, ffn_wi, ffn_wo, 2 * l)
        if l < n_a:
            vres = None
            if l > 0:
                vres = (row(rwkv_v0[l - 1]), rwkv_v1[l - 1].astype(BF16),
                        rwkv_v2[l - 1].astype(BF16), v_first)
            r, ld, k, v, na, nb, g = _rwkv_proj(
                xs, row(mix_norm[l]), rwkv_mu[l], rwkv_w_rkv[l].astype(BF16), row(rwkv_w0[l]),
                rwkv_w1[l].astype(BF16), rwkv_w2[l].astype(BF16), row(rwkv_a0[l]),
                rwkv_a1[l].astype(BF16), rwkv_a2[l].astype(BF16), rwkv_g1[l].astype(BF16),
                rwkv_g2[l].astype(BF16), row(rwkv_k_k[l]), row(rwkv_k_a[l]), e_red, e_exp, vres)
            if l == 0:
                v_first = v
            y = _rwkv_scan(r, ld, k, v, na, nb)
            xs = _rwkv_post(y, r, k, v, g, xs, row(rwkv_ln_w[l]), row(rwkv_ln_b[l]),
                            row(rwkv_r_k[l]), rwkv_w_o[l].astype(BF16), e_red, e_exp)
        else:
            i = l - n_a
            q_gain = jnp.tile(q_norm[i], (1, d // HEAD_DIM)).reshape(n_groups, 1, d)
            qs = _q_proj(xs, row(mix_norm[l]), attn_w_q[i].astype(BF16), q_gain, e_red, e_exp)
            parts = [_dil_attn(qs[grp], kvs[grp], DIL_GROUPS[grp][1]) for grp in range(n_groups)]
            xs = _attn_combine(parts, xs, attn_w_o[i].astype(BF16), e_exp)
        xs = _ffn(xs, ffn_g, ffn_wi, ffn_wo, 2 * l + 1)
        if l == n_a - 1:
            k_gain = jnp.tile(k_norm, (1, d // HEAD_DIM)).reshape(n_groups, 1, d)
            kvs = _kv_proj(xs, row(kv_norm), w_kv.astype(BF16), k_gain, e_red, e_exp)
    return xs.reshape(bsz, t, d)
```

```python
import functools

import jax
import jax.numpy as jnp
from jax import lax
from jax.experimental import pallas as pl
from jax.experimental.pallas import tpu as pltpu

F32 = jnp.float32
BF16 = jnp.bfloat16

RMS_EPS = 1e-6
GN_EPS = 64e-5
HEAD_DIM = 64
LANES = 128
DIL_GROUPS = ((128, 1), (512, 4), (2048, 16))
ATT_BLK = 128
ATT_SUB = 4
ATT_WAVE = 1
NEG_INF = -1e30
LOG2_E = 1.4426950408889634
CHUNK = 64
SCAN_BLOCK = 512
TOKEN_TILE = 512
FFN_COL_TILE = 256
PROJ_COL_TILE = 256
VMEM_LIMIT = 56 * 1024 * 1024


def _params(*sem):
    return pltpu.CompilerParams(dimension_semantics=sem, vmem_limit_bytes=VMEM_LIMIT)


def _mm(a, b):
    return jnp.dot(a.astype(BF16), b.astype(BF16), preferred_element_type=F32)


def _mm_nt(a, b):
    return lax.dot_general(a.astype(BF16), b.astype(BF16), (((1,), (1,)), ((), ())),
                           preferred_element_type=F32)


def _rms(x, g):
    return x * lax.rsqrt(jnp.mean(x * x, axis=-1, keepdims=True) + RMS_EPS) * g


def _sigmoid(x):
    return 1.0 / (1.0 + jnp.exp(-x))


def _full(shape):
    n = len(shape)
    return pl.BlockSpec(shape, lambda *_: (0,) * n)


def _resident(shape):
    n = len(shape)
    return pl.BlockSpec(shape, lambda *_: (0,) * n, pipeline_mode=pl.Buffered(1))


def _ffn_kernel(x_ref, g_ref, win_ref, wout_ref, o_ref, act_ref):
    x = x_ref[...]
    hn = _rms(x, g_ref[...]).astype(BF16)
    d_ff = wout_ref.shape[0]
    for c in range(d_ff // FFN_COL_TILE):
        lo = c * FFN_COL_TILE
        gate = jnp.dot(hn, win_ref[:, lo:lo + FFN_COL_TILE].astype(BF16),
                       preferred_element_type=F32)
        up = jnp.dot(hn, win_ref[:, d_ff + lo:d_ff + lo + FFN_COL_TILE].astype(BF16),
                     preferred_element_type=F32)
        act_ref[:, lo:lo + FFN_COL_TILE] = (gate * _sigmoid(gate) * up).astype(BF16)
    o_ref[...] = x + 0.5 * jnp.dot(act_ref[...], wout_ref[...].astype(BF16),
                                   preferred_element_type=F32)


def _ffn(x, norms, w_in, w_out, idx):
    t, d = x.shape
    d_ff = w_out.shape[1]
    tm = TOKEN_TILE
    return pl.pallas_call(
        _ffn_kernel,
        grid=(t // tm,),
        in_specs=[
            pl.BlockSpec((tm, d), lambda i: (i, 0)),
            pl.BlockSpec((None, 1, d), lambda i: (idx, 0, 0)),
            pl.BlockSpec((None, d, 2 * d_ff), lambda i: (idx, 0, 0), pipeline_mode=pl.Buffered(1)),
            pl.BlockSpec((None, d_ff, d), lambda i: (idx, 0, 0), pipeline_mode=pl.Buffered(1)),
        ],
        out_specs=pl.BlockSpec((tm, d), lambda i: (i, 0)),
        out_shape=jax.ShapeDtypeStruct((t, d), F32),
        scratch_shapes=[pltpu.VMEM((tm, d_ff), BF16)],
        compiler_params=_params("parallel"),
        name="ffn",
    )(x, norms, w_in, w_out)


def _head_sum(x, e_red):
    return jnp.dot(x.astype(BF16), e_red, preferred_element_type=F32)


def _spread(c, e_exp):
    hi = c.astype(BF16)
    lo = (c - hi.astype(F32)).astype(BF16)
    return jnp.dot(jnp.concatenate([hi, lo], axis=1), e_exp, preferred_element_type=F32)


def _rwkv_proj_kernel(*refs, has_vres):
    if has_vres:
        (x_ref, xp_ref, gn_ref, mu_ref, wrkv_ref, w0_ref, w1_ref, w2_ref, a0_ref, a1_ref,
         a2_ref, g1_ref, g2_ref, kk_ref, ka_ref, ered_ref, eexp_ref, v0_ref, v1_ref, v2_ref,
         vf_ref, r_o, ld_o, k_o, v_o, na_o, nb_o, g_o) = refs
    else:
        (x_ref, xp_ref, gn_ref, mu_ref, wrkv_ref, w0_ref, w1_ref, w2_ref, a0_ref, a1_ref,
         a2_ref, g1_ref, g2_ref, kk_ref, ka_ref, ered_ref, eexp_ref,
         r_o, ld_o, k_o, v_o, na_o, nb_o, g_o) = refs
    i = pl.program_id(0)
    gn = gn_ref[...]
    h = _rms(x_ref[...], gn)
    hp = _rms(xp_ref[7:8, :], gn) * jnp.where(i > 0, 1.0, 0.0)
    rows = lax.broadcasted_iota(jnp.int32, h.shape, 0)
    hs = jnp.where(rows == 0, hp, pltpu.roll(h, 1, axis=0))
    xx = hs - h
    mu = mu_ref[...]
    xr, xw, xk, xv, xa, xg = [(h + xx * mu[j:j + 1]).astype(BF16) for j in range(6)]
    w_dn = jnp.tanh(_mm(xw, w1_ref[...])).astype(BF16)
    a_dn = _mm(xa, a1_ref[...]).astype(BF16)
    g_dn = _sigmoid(_mm(xg, g1_ref[...])).astype(BF16)
    v_dn = _mm(xv, v1_ref[...]).astype(BF16) if has_vres else None
    n_col = PROJ_COL_TILE
    for c0 in range(0, h.shape[1], n_col):
        cols = slice(c0, c0 + n_col)
        mm = lambda a, w_ref: jnp.dot(a, w_ref[:, cols], preferred_element_type=F32)
        r = jnp.dot(xr, wrkv_ref[0, :, cols], preferred_element_type=F32)
        k = jnp.dot(xk, wrkv_ref[1, :, cols], preferred_element_type=F32)
        v = jnp.dot(xv, wrkv_ref[2, :, cols], preferred_element_type=F32)
        w = w0_ref[:, cols] + mm(w_dn, w2_ref)
        z = -w
        wl = -(jnp.maximum(z, 0.0) + jnp.log(1.0 + jnp.exp(-jnp.abs(z)))) - 0.5
        ld_o[:, cols] = -jnp.exp(wl)
        a = _sigmoid(a0_ref[:, cols] + mm(a_dn, a2_ref))
        g_o[:, cols] = mm(g_dn, g2_ref).astype(BF16)
        if has_vres:
            v = v + (vf_ref[:, cols].astype(F32) - v) * _sigmoid(v0_ref[:, cols] + mm(v_dn, v2_ref))
        kkr = k * kk_ref[:, cols]
        norm = jnp.sqrt(_head_sum(kkr * kkr, ered_ref[cols, :]))
        kk = kkr * _spread(1.0 / jnp.maximum(norm, 1e-12), eexp_ref[:, cols])
        r_o[:, cols] = r.astype(BF16)
        k_o[:, cols] = (k * (1.0 + (a - 1.0) * ka_ref[:, cols])).astype(BF16)
        v_o[:, cols] = v.astype(BF16)
        na_o[:, cols] = (-kk).astype(BF16)
        nb_o[:, cols] = (kk * a).astype(BF16)


def _rwkv_proj(x, gn, mu, wrkv, w0, w1, w2, a0, a1, a2, g1, g2, k_k, k_a, e_red, e_exp, vres):
    t, d = x.shape
    tm = TOKEN_TILE
    row = pl.BlockSpec((tm, d), lambda i: (i, 0))
    prev = pl.BlockSpec((8, d), lambda i: (jnp.maximum(i * (tm // 8) - 1, 0), 0))
    args = [x, x, gn, mu, wrkv, w0, w1, w2, a0, a1, a2, g1, g2, k_k, k_a, e_red, e_exp]
    specs = [row, prev, _full(gn.shape), _full(mu.shape), _resident(wrkv.shape), _full(w0.shape),
             _full(w1.shape), _full(w2.shape), _full(a0.shape), _full(a1.shape), _full(a2.shape),
             _full(g1.shape), _full(g2.shape), _full(k_k.shape), _full(k_a.shape),
             _full(e_red.shape), _full(e_exp.shape)]
    if vres is not None:
        v0, v1, v2, v_first = vres
        args += [v0, v1, v2, v_first]
        specs += [_full(v0.shape), _full(v1.shape), _full(v2.shape), row]
    return pl.pallas_call(
        functools.partial(_rwkv_proj_kernel, has_vres=vres is not None),
        grid=(t // tm,),
        in_specs=specs,
        out_specs=[row] * 7,
        out_shape=[jax.ShapeDtypeStruct((t, d), dt) for dt in (BF16, F32, BF16, BF16, BF16, BF16, BF16)],
        compiler_params=_params("parallel"),
        name="rwkv_proj",
    )(*args)


def _interleave(primary, filler):
    out = []
    for i in range(max(len(primary), len(filler))):
        if i < len(filler):
            out.append(filler[i])
        if i < len(primary):
            out.append(primary[i])
    return out


def _rwkv_scan_kernel(r_ref, ld_ref, k_ref, v_ref, na_ref, nb_ref, y_ref, s_ref):
    n_pairs = r_ref.shape[1] // LANES
    n_chunks = r_ref.shape[0] // CHUNK
    L = CHUNK
    assert 2 * L == LANES and L == HEAD_DIM and n_chunks % 2 == 0

    @pl.when(pl.program_id(0) == 0)
    def _():
        s_ref[...] = jnp.zeros_like(s_ref)

    ri = lax.broadcasted_iota(jnp.int32, (2 * L, 2 * L), 0)
    ci = lax.broadcasted_iota(jnp.int32, (2 * L, 2 * L), 1)
    same_head = (ri < L) == (ci < L)
    other_head = jnp.logical_not(same_head)
    strict = (ci & (L - 1)) < (ri & (L - 1))
    incl = (ci & (L - 1)) <= (ri & (L - 1))
    eye = (ri == ci).astype(F32)
    head0 = lax.broadcasted_iota(jnp.int32, (L, LANES), 1) < HEAD_DIM
    first_head = lax.broadcasted_iota(jnp.int32, (2 * L, LANES), 1) < HEAD_DIM
    tri = (lax.broadcasted_iota(jnp.int32, (L, L), 0)
           >= lax.broadcasted_iota(jnp.int32, (L, L), 1)).astype(BF16)
    zero = jnp.zeros((2 * L, LANES), BF16)
    pairs = range(n_pairs)
    lanes_of = lambda a, p: a[:, p * LANES:(p + 1) * LANES]
    stack2 = lambda a, b: jnp.concatenate([a, b], axis=0)

    state = [s_ref[p] for p in pairs]

    def prep_stages(chunks):
        items = [(c, p) for c in chunks for p in pairs]
        d = {}

        def elementwise():
            for c in chunks:
                rows = slice(c * L, (c + 1) * L)
                f32 = lambda ref: ref[rows, :].astype(F32)
                r, ld, k = f32(r_ref), ld_ref[rows, :], f32(k_ref)
                v, na, nb = f32(v_ref), f32(na_ref), f32(nb_ref)
                l1 = ld.astype(BF16)
                rem = ld - l1.astype(F32)
                l2 = rem.astype(BF16)
                l3 = (rem - l2.astype(F32)).astype(BF16)
                tri_dot = lambda a: jnp.dot(tri, a, preferred_element_type=F32)
                cum = tri_dot(l1) + (tri_dot(l2) + tri_dot(l3))
                cum_last = cum[L - 1:L, :]
                w_inv = jnp.exp(-cum)
                w_rel = jnp.exp(cum_last - cum)
                d["w_last", c] = jnp.exp(cum_last)
                d["x", c] = stack2(r * jnp.exp(cum), na * jnp.exp(cum - ld)).astype(BF16)
                d["bk", c] = stack2(nb * w_inv, k * w_inv).astype(BF16)
                d["kb", c] = stack2(k * w_inv, nb * w_inv).astype(BF16)
                d["keys", c] = stack2(nb * w_rel, k * w_rel).astype(BF16)
                d["vv", c] = stack2(v, v).astype(BF16)
                d["v", c] = v

        def gram():
            for c, p in items:
                rhs = stack2(jnp.where(first_head, lanes_of(d["bk", c], p), zero),
                             jnp.where(first_head, zero, lanes_of(d["kb", c], p)))
                d["g", c, p] = _mm_nt(lanes_of(d["x", c], p), rhs)

        def split():
            for c, p in items:
                g = d.pop(("g", c, p))
                g0, g1 = g[:, :LANES], g[:, LANES:]
                qs = stack2(g0[L:], g1[L:])
                ms = stack2(g0[:L], g1[:L])
                nil = jnp.where(same_head & strict, qs, 0.0)
                d["m_uv", c, p] = jnp.concatenate(
                    [jnp.where(same_head & incl, ms, 0.0),
                     jnp.where(other_head & incl, ms, 0.0)], axis=1).astype(BF16)
                d["akv", c, p] = _mm(jnp.where(other_head & strict, qs, 0.0),
                                     lanes_of(d["vv", c], p))
                d["t", c, p] = eye + nil
                d["pw", c, p] = _mm(nil, nil)

        def double():
            for c, p in items:
                pw = d["pw", c, p]
                t = d["t", c, p]
                both = _mm(pw, jnp.concatenate([t, pw], axis=1))
                d["t", c, p] = t + both[:, :LANES]
                d["pw", c, p] = both[:, LANES:]

        def finish():
            for c, p in items:
                t = d["t", c, p]
                d["t", c, p] = (t + _mm(d.pop(("pw", c, p)), t)).astype(BF16)

        return [elementwise, gram, split, double, double, double, double, finish], d

    def chunk_stages(c, d):
        loc = {}

        def project():
            for p in pairs:
                loc["xs", p] = _mm_nt(lanes_of(d["x", c], p), state[p])

        def solve():
            for p in pairs:
                xs = loc["xs", p]
                u = _mm(d["t", c, p], stack2(xs[L:], xs[L:]) + d["akv", c, p])
                loc["u", p] = jnp.where(head0, u[:L], u[L:])

        def emit():
            for p in pairs:
                xs, u = loc["xs", p], loc["u", p]
                ub = u.astype(BF16)
                vvp = lanes_of(d["vv", c], p)
                y = stack2(xs[:L], xs[:L]) + _mm(d["m_uv", c, p],
                                                 jnp.concatenate([ub, ub, vvp], axis=0))
                y_ref[c * L:(c + 1) * L, p * LANES:(p + 1) * LANES] = jnp.where(
                    head0, y[:L], y[L:]).astype(y_ref.dtype)
                vals = stack2(u, lanes_of(d["v", c], p))
                upd = _mm(vals.T, lanes_of(d["keys", c], p))
                state[p] = jnp.where(
                    same_head, state[p] * lanes_of(d["w_last", c], p) + upd, 0.0)

        return [project, solve, emit]

    groups = [(c, c + 1) for c in range(0, n_chunks, 2)]
    prep, data = prep_stages(groups[0])
    for stage in prep:
        stage()
    for gi, grp in enumerate(groups):
        chain = chunk_stages(grp[0], data) + chunk_stages(grp[1], data)
        if gi + 1 < len(groups):
            prep, nxt = prep_stages(groups[gi + 1])
        else:
            prep, nxt = [], None
        for stage in _interleave(chain, prep):
            stage()
        data = nxt
    for p in pairs:
        s_ref[p] = state[p]


def _rwkv_scan(r, ld, k, v, na, nb):
    t, d = r.shape
    blk = pl.BlockSpec((SCAN_BLOCK, d), lambda c: (c, 0))
    return pl.pallas_call(
        _rwkv_scan_kernel,
        grid=(t // SCAN_BLOCK,),
        in_specs=[blk] * 6,
        out_specs=blk,
        out_shape=jax.ShapeDtypeStruct((t, d), BF16),
        scratch_shapes=[pltpu.VMEM((d // LANES, LANES, LANES), F32)],
        compiler_params=_params("arbitrary"),
        name="rwkv_scan",
    )(r, ld, k, v, na, nb)


def _rwkv_post_kernel(y_ref, r_ref, k_ref, v_ref, g_ref, x_ref, lnw_ref, lnb_ref, rk_ref,
                      wo_ref, ered_ref, eexp_ref, o_ref):
    e_red = ered_ref[...]
    e_exp = eexp_ref[...]
    inv_n = 1.0 / HEAD_DIM
    n_sub = 4
    sub = y_ref.shape[0] // n_sub
    blocks = [slice(s * sub, (s + 1) * sub) for s in range(n_sub)]
    ys = [y_ref[b, :].astype(F32) for b in blocks]
    sum_y = [_head_sum(y, e_red) for y in ys]
    sum_b = [_head_sum(r_ref[b, :].astype(F32) * k_ref[b, :].astype(F32) * rk_ref[...], e_red)
             for b in blocks]
    yc = [y - _spread(s * inv_n, e_exp) for y, s in zip(ys, sum_y)]
    bonus = [_spread(s, e_exp) * v_ref[b, :].astype(F32) for s, b in zip(sum_b, blocks)]
    var = [_head_sum(c * c, e_red) * inv_n for c in yc]
    rstd = [_spread(lax.rsqrt(v + GN_EPS), e_exp) for v in var]
    z = [((c * rs * lnw_ref[...] + lnb_ref[...]) + bn) * g_ref[b, :].astype(F32)
         for c, rs, bn, b in zip(yc, rstd, bonus, blocks)]
    o_ref[...] = x_ref[...] + _mm(jnp.concatenate(z, axis=0), wo_ref[...])


def _rwkv_post(y, r, k, v, g, x, ln_w, ln_b, r_k, w_o, e_red, e_exp):
    t, d = x.shape
    tm = TOKEN_TILE
    row = pl.BlockSpec((tm, d), lambda i: (i, 0))
    return pl.pallas_call(
        _rwkv_post_kernel,
        grid=(t // tm,),
        in_specs=[row] * 6 + [_full(ln_w.shape), _full(ln_b.shape), _full(r_k.shape),
                              _resident(w_o.shape), _full(e_red.shape), _full(e_exp.shape)],
        out_specs=row,
        out_shape=jax.ShapeDtypeStruct((t, d), F32),
        compiler_params=_params("parallel"),
        name="rwkv_post",
    )(y, r, k, v, g, x, ln_w, ln_b, r_k, w_o, e_red, e_exp)


def _store_dilated(dst_ref, y, stage_ref, dil):
    rows, d = y.shape
    if dil == 1:
        dst_ref[...] = y.astype(dst_ref.dtype)
        return
    n_tiles = d // LANES
    for c in range(n_tiles):
        stage_ref[c] = y[:, c * LANES:(c + 1) * LANES]
    for r in range(dil):
        for c in range(n_tiles):
            lo = r * d + c * LANES
            dst_ref[:, lo:lo + LANES] = (
                stage_ref[c, pl.ds(r, rows // dil, stride=dil), :].astype(dst_ref.dtype))


def _proj_blocks(hn, w_ref, n_normed, gain_ref, e_red, e_exp, emit):
    d = hn.shape[1]
    n_blk = w_ref.shape[1] // d
    normed = sorted(range(n_normed), reverse=True)
    order = normed + sorted(range(n_normed, n_blk), reverse=True)
    ys, ms = {}, {}

    def project(b):
        ys[b] = jnp.dot(hn, w_ref[:, b * d:(b + 1) * d].astype(BF16),
                        preferred_element_type=F32)

    def head_sums(b):
        if b in normed:
            ms[b] = _head_sum(ys[b] * ys[b], e_red) * (1.0 / HEAD_DIM)

    def finish(b):
        y = ys.pop(b)
        if b in normed:
            y = y * _spread(lax.rsqrt(ms.pop(b) + RMS_EPS), e_exp) * gain_ref[b]
        emit(b, y)

    for i in range(n_blk + 2):
        for lag, stage in enumerate((project, head_sums, finish)):
            if 0 <= i - lag < n_blk:
                stage(order[i - lag])


def _kv_proj_kernel(x_ref, gn_ref, w_ref, gain_ref, ered_ref, eexp_ref, o0_ref, o1_ref, o2_ref,
                    *stage_refs):
    n_groups = len(DIL_GROUPS)
    out_refs = (o0_ref, o1_ref, o2_ref)
    hn = _rms(x_ref[...], gn_ref[...]).astype(BF16)

    def emit(b, y):
        is_value, grp = divmod(b, n_groups)
        _store_dilated(out_refs[grp].at[is_value], y, stage_refs[grp], DIL_GROUPS[grp][1])

    _proj_blocks(hn, w_ref, n_groups, gain_ref, ered_ref[...], eexp_ref[...], emit)


def _kv_proj(x, gn, w_kv, gains, e_red, e_exp):
    t, d = x.shape
    tm = TOKEN_TILE
    return pl.pallas_call(
        _kv_proj_kernel,
        grid=(t // tm,),
        in_specs=[pl.BlockSpec((tm, d), lambda i: (i, 0)), _full(gn.shape),
                  _resident(w_kv.shape), _full(gains.shape), _full(e_red.shape),
                  _full(e_exp.shape)],
        out_specs=[pl.BlockSpec((2, tm // dil, dil * d), lambda i: (0, i, 0))
                   for _, dil in DIL_GROUPS],
        out_shape=[jax.ShapeDtypeStruct((2, t // dil, dil * d), BF16) for _, dil in DIL_GROUPS],
        scratch_shapes=[pltpu.VMEM((d // LANES, tm, LANES), F32) for _ in DIL_GROUPS],
        compiler_params=_params("parallel"),
        name="kv_proj",
    )(x, gn, w_kv, gains, e_red, e_exp)


def _q_proj_kernel(x_ref, gn_ref, w_ref, gain_ref, ered_ref, eexp_ref, o0_ref, o1_ref, o2_ref,
                   *stage_refs):
    out_refs = (o0_ref, o1_ref, o2_ref)
    hn = _rms(x_ref[...], gn_ref[...]).astype(BF16)
    scale = HEAD_DIM ** -0.5 * LOG2_E

    def emit(grp, y):
        _store_dilated(out_refs[grp], y * scale, stage_refs[grp], DIL_GROUPS[grp][1])

    _proj_blocks(hn, w_ref, len(DIL_GROUPS), gain_ref, ered_ref[...], eexp_ref[...], emit)


def _q_proj(x, gn, w_q, gains, e_red, e_exp):
    t, d = x.shape
    tm = TOKEN_TILE
    return pl.pallas_call(
        _q_proj_kernel,
        grid=(t // tm,),
        in_specs=[pl.BlockSpec((tm, d), lambda i: (i, 0)), _full(gn.shape),
                  _resident(w_q.shape), _full(gains.shape), _full(e_red.shape),
                  _full(e_exp.shape)],
        out_specs=[pl.BlockSpec((tm // dil, dil * d), lambda i: (i, 0)) for _, dil in DIL_GROUPS],
        out_shape=[jax.ShapeDtypeStruct((t // dil, dil * d), BF16) for _, dil in DIL_GROUPS],
        scratch_shapes=[pltpu.VMEM((d // LANES, tm, LANES), F32) for _ in DIL_GROUPS],
        compiler_params=_params("parallel"),
        name="q_proj",
    )(x, gn, w_q, gains, e_red, e_exp)


def _dil_attn_kernel(q_ref, kp_ref, kc_ref, vp_ref, vc_ref, o_ref, m_ref, l_ref):
    n = pl.program_id(1)
    blk = ATT_BLK
    n_pairs = q_ref.shape[1] // LANES
    n_sub = q_ref.shape[0] // blk
    qi = lax.broadcasted_iota(jnp.int32, (2 * blk, 2 * blk), 0) & (blk - 1)
    kj = lax.broadcasted_iota(jnp.int32, (2 * blk, 2 * blk), 1)
    band = (kj >= qi) & (kj <= qi + blk)
    valid = [band & ((n > 0) | (kj >= blk))] + [band] * (n_sub - 1)
    lane = lax.broadcasted_iota(jnp.int32, (blk, LANES), 1)
    head0 = lane < HEAD_DIM
    dn_nt = (((1,), (1,)), ((), ()))
    zero = jnp.zeros((blk, LANES), BF16)
    lanes_of = lambda a, p: a[:, p * LANES:(p + 1) * LANES]

    def keys_of(prev_ref, cur_ref, s, p):
        if s == 0:
            return jnp.concatenate([lanes_of(prev_ref, p), lanes_of(cur_ref, p)[:blk]], axis=0)
        return lanes_of(cur_ref, p)[(s - 1) * blk:(s + 1) * blk]

    waves = [(s, range(p0, p0 + ATT_WAVE)) for s in range(n_sub)
             for p0 in range(0, n_pairs, ATT_WAVE)]
    scores, probs = {}, {}
    m_acc = [jnp.zeros((blk, LANES), F32) for _ in range(n_sub)]
    l_acc = [jnp.ones((blk, LANES), F32) for _ in range(n_sub)]

    def score_stage(s, ps):
        for p in ps:
            qp = lanes_of(q_ref, p)[s * blk:(s + 1) * blk]
            q2 = jnp.concatenate([jnp.where(head0, qp, zero), jnp.where(head0, zero, qp)], axis=0)
            scores[s, p] = lax.dot_general(q2, keys_of(kp_ref, kc_ref, s, p), dn_nt,
                                           preferred_element_type=F32)

    def softmax_stage(s, ps):
        for p in ps:
            sc = jnp.where(valid[s], scores.pop((s, p)), NEG_INF)
            mx = jnp.max(sc, axis=-1, keepdims=True)
            pe = jnp.exp2(sc - mx)
            ls = jnp.sum(pe, axis=-1, keepdims=True)
            for hh in range(2):
                hit = lane == 2 * p + hh
                rows = slice(hh * blk, (hh + 1) * blk)
                m_acc[s] = jnp.where(hit, mx[rows], m_acc[s])
                l_acc[s] = jnp.where(hit, ls[rows], l_acc[s])
            probs[s, p] = pe.astype(BF16)

    def output_stage(s, ps):
        for p in ps:
            o2 = jnp.dot(probs.pop((s, p)), keys_of(vp_ref, vc_ref, s, p),
                         preferred_element_type=F32)
            o_ref[s * blk:(s + 1) * blk, p * LANES:(p + 1) * LANES] = jnp.where(
                head0, o2[:blk], o2[blk:]).astype(o_ref.dtype)

    for i in range(len(waves) + 2):
        for lag, stage in enumerate((score_stage, softmax_stage, output_stage)):
            if 0 <= i - lag < len(waves):
                stage(*waves[i - lag])
    for s in range(n_sub):
        m_ref[s * blk:(s + 1) * blk, :] = m_acc[s]
        l_ref[s * blk:(s + 1) * blk, :] = l_acc[s]


def _dil_attn(q, kv, dil):
    m_rows, wide = q.shape
    d = wide // dil
    blk = ATT_BLK
    step = ATT_SUB * blk
    cur = lambda a: pl.BlockSpec((None, step, d), lambda r, n: (a, n, r))
    prev = lambda a: pl.BlockSpec((None, blk, d),
                                  lambda r, n: (a, jnp.maximum(ATT_SUB * n - 1, 0), r))
    return pl.pallas_call(
        _dil_attn_kernel,
        grid=(dil, m_rows // step),
        in_specs=[pl.BlockSpec((step, d), lambda r, n: (n, r)), prev(0), cur(0), prev(1), cur(1)],
        out_specs=[pl.BlockSpec((step, d), lambda r, n: (n, r)),
                   pl.BlockSpec((step, LANES), lambda r, n: (n, r)),
                   pl.BlockSpec((step, LANES), lambda r, n: (n, r))],
        out_shape=[jax.ShapeDtypeStruct((m_rows, dil * d), BF16),
                   jax.ShapeDtypeStruct((m_rows, dil * LANES), F32),
                   jax.ShapeDtypeStruct((m_rows, dil * LANES), F32)],
        compiler_params=_params("parallel", "parallel"),
        name="dil_attn",
    )(q, kv, kv, kv, kv)


def _load_dilated(src_ref, stage_ref, dil):
    if dil == 1:
        return src_ref[...]
    sub = src_ref.shape[0]
    n_tiles = src_ref.shape[1] // dil // LANES
    for r in range(dil):
        for c in range(n_tiles):
            lo = (r * n_tiles + c) * LANES
            stage_ref[c, pl.ds(r, sub, stride=dil), :] = src_ref[:, lo:lo + LANES].astype(F32)
    return jnp.concatenate([stage_ref[c] for c in range(n_tiles)], axis=1)


def _attn_combine_kernel(o0, o1, o2, m0, m1, m2, l0, l1, l2, x_ref, wo_ref, eexp_ref, out_ref,
                         os1, os2, ms1, ms2, ls1, ls2):
    dils = [dil for _, dil in DIL_GROUPS]
    ms = [_load_dilated(ref, st, dil) for ref, st, dil in zip((m0, m1, m2), (None, ms1, ms2), dils)]
    ls = [_load_dilated(ref, st, dil) for ref, st, dil in zip((l0, l1, l2), (None, ls1, ls2), dils)]
    top = jnp.maximum(jnp.maximum(ms[0], ms[1]), ms[2])
    es = [jnp.exp2(mg - top) for mg in ms]
    den = es[0] * ls[0] + es[1] * ls[1] + es[2] * ls[2]
    e_exp = eexp_ref[...]
    acc = None
    for eg, ref, st, dil in zip(es, (o0, o1, o2), (None, os1, os2), dils):
        term = _spread(eg / den, e_exp) * _load_dilated(ref, st, dil)
        acc = term if acc is None else acc + term
    out_ref[...] = x_ref[...] + _mm(acc, wo_ref[...])


def _attn_combine(parts, x, w_o, e_exp):
    t, d = x.shape
    tm = TOKEN_TILE
    row = pl.BlockSpec((tm, d), lambda i: (i, 0))
    dils = [dil for _, dil in DIL_GROUPS]
    o_specs = [pl.BlockSpec((tm // dil, dil * d), lambda i: (i, 0)) for dil in dils]
    s_specs = [pl.BlockSpec((tm // dil, dil * LANES), lambda i: (i, 0)) for dil in dils]
    os_, ms_, ls_ = zip(*parts)
    return pl.pallas_call(
        _attn_combine_kernel,
        grid=(t // tm,),
        in_specs=o_specs + s_specs + s_specs + [row, _resident(w_o.shape), _full(e_exp.shape)],
        out_specs=row,
        out_shape=jax.ShapeDtypeStruct((t, d), F32),
        scratch_shapes=([pltpu.VMEM((d // LANES, tm, LANES), F32)] * 2
                        + [pltpu.VMEM((1, tm, LANES), F32)] * 4),
        compiler_params=_params("parallel"),
        name="attn_combine",
    )(*os_, *ms_, *ls_, x, w_o, e_exp)


def kernel(x, ffn_norm, ffn_w_in, ffn_w_out, mix_norm, rwkv_mu, rwkv_w_rkv, rwkv_w0, rwkv_w1, rwkv_w2, rwkv_a0, rwkv_a1, rwkv_a2, rwkv_v0, rwkv_v1, rwkv_v2, rwkv_g1, rwkv_g2, rwkv_k_k, rwkv_k_a, rwkv_r_k, rwkv_ln_w, rwkv_ln_b, rwkv_w_o, kv_norm, w_kv, k_norm, attn_w_q, q_norm, attn_w_o):
    bsz, t, d = x.shape
    assert bsz == 1 and d % LANES == 0 and t % max(w for w, _ in DIL_GROUPS) == 0
    assert all(w // dil == ATT_BLK for w, dil in DIL_GROUPS)
    depth = ffn_norm.shape[0]
    n_a = rwkv_mu.shape[0]
    n_groups = len(DIL_GROUPS)
    row = lambda p: p.reshape(1, -1)

    ffn_g = ffn_norm.reshape(2 * depth, 1, d)
    ffn_wi = ffn_w_in.reshape(2 * depth, d, -1)
    ffn_wo = ffn_w_out.reshape(2 * depth, -1, d)
    head_of_lane = jnp.arange(d) // HEAD_DIM
    one_hot = (jnp.arange(LANES)[:, None] == head_of_lane[None, :]).astype(BF16)
    e_red = one_hot.T
    e_exp = jnp.concatenate([one_hot, one_hot], axis=0)

    xs = x.reshape(t, d)
    v_first = None
    kvs = None
    for l in range(depth):
        xs = _ffn(xs, ffn_g, ffn_wi, ffn_wo, 2 * l)
        if l < n_a:
            vres = None
            if l > 0:
                vres = (row(rwkv_v0[l - 1]), rwkv_v1[l - 1].astype(BF16),
                        rwkv_v2[l - 1].astype(BF16), v_first)
            r, ld, k, v, na, nb, g = _rwkv_proj(
                xs, row(mix_norm[l]), rwkv_mu[l], rwkv_w_rkv[l].astype(BF16), row(rwkv_w0[l]),
                rwkv_w1[l].astype(BF16), rwkv_w2[l].astype(BF16), row(rwkv_a0[l]),
                rwkv_a1[l].astype(BF16), rwkv_a2[l].astype(BF16), rwkv_g1[l].astype(BF16),
                rwkv_g2[l].astype(BF16), row(rwkv_k_k[l]), row(rwkv_k_a[l]), e_red, e_exp, vres)
            if l == 0:
                v_first = v
            y = _rwkv_scan(r, ld, k, v, na, nb)
            xs = _rwkv_post(y, r, k, v, g, xs, row(rwkv_ln_w[l]), row(rwkv_ln_b[l]),
                            row(rwkv_r_k[l]), rwkv_w_o[l].astype(BF16), e_red, e_exp)
        else:
            i = l - n_a
            q_gain = jnp.tile(q_norm[i], (1, d // HEAD_DIM)).reshape(n_groups, 1, d)
            qs = _q_proj(xs, row(mix_norm[l]), attn_w_q[i], q_gain, e_red, e_exp)
            parts = [_dil_attn(qs[grp], kvs[grp], DIL_GROUPS[grp][1]) for grp in range(n_groups)]
            xs = _attn_combine(parts, xs, attn_w_o[i].astype(BF16), e_exp)
        xs = _ffn(xs, ffn_g, ffn_wi, ffn_wo, 2 * l + 1)
        if l == n_a - 1:
            k_gain = jnp.tile(k_norm, (1, d // HEAD_DIM)).reshape(n_groups, 1, d)
            kvs = _kv_proj(xs, row(kv_norm), w_kv.astype(BF16), k_gain, e_red, e_exp)
    return xs.reshape(bsz, t, d)
```

```python
import functools

import jax
import jax.numpy as jnp
from jax import lax
from jax.experimental import pallas as pl
from jax.experimental.pallas import tpu as pltpu

F32 = jnp.float32
BF16 = jnp.bfloat16

RMS_EPS = 1e-6
GN_EPS = 64e-5
HEAD_DIM = 64
LANES = 128
DIL_GROUPS = ((128, 1), (512, 4), (2048, 16))
ATT_BLK = 128
ATT_SUB = 4
ATT_WAVE = 1
NEG_INF = -1e30
LOG2_E = 1.4426950408889634
CHUNK = 64
SCAN_BLOCK = 512
TOKEN_TILE = 512
FFN_COL_TILE = 256
PROJ_COL_TILE = 256
VMEM_LIMIT = 56 * 1024 * 1024


def _params(*sem):
    return pltpu.CompilerParams(dimension_semantics=sem, vmem_limit_bytes=VMEM_LIMIT)


def _mm(a, b):
    return jnp.dot(a.astype(BF16), b.astype(BF16), preferred_element_type=F32)


def _mm_nt(a, b):
    return lax.dot_general(a.astype(BF16), b.astype(BF16), (((1,), (1,)), ((), ())),
                           preferred_element_type=F32)


def _rms(x, g):
    return x * lax.rsqrt(jnp.mean(x * x, axis=-1, keepdims=True) + RMS_EPS) * g


def _sigmoid(x):
    return 1.0 / (1.0 + jnp.exp(-x))


def _full(shape):
    n = len(shape)
    return pl.BlockSpec(shape, lambda *_: (0,) * n)


def _resident(shape):
    n = len(shape)
    return pl.BlockSpec(shape, lambda *_: (0,) * n, pipeline_mode=pl.Buffered(1))


def _ffn_kernel(x_ref, g_ref, win_ref, wout_ref, o_ref, act_ref):
    x = x_ref[...]
    hn = _rms(x, g_ref[...]).astype(BF16)
    d_ff = wout_ref.shape[0]
    for c in range(d_ff // FFN_COL_TILE):
        lo = c * FFN_COL_TILE
        gate = jnp.dot(hn, win_ref[:, lo:lo + FFN_COL_TILE].astype(BF16),
                       preferred_element_type=F32)
        up = jnp.dot(hn, win_ref[:, d_ff + lo:d_ff + lo + FFN_COL_TILE].astype(BF16),
                     preferred_element_type=F32)
        act_ref[:, lo:lo + FFN_COL_TILE] = (gate * _sigmoid(gate) * up).astype(BF16)
    o_ref[...] = x + 0.5 * jnp.dot(act_ref[...], wout_ref[...].astype(BF16),
                                   preferred_element_type=F32)


def _ffn(x, norms, w_in, w_out, idx):
    t, d = x.shape
    d_ff = w_out.shape[1]
    tm = TOKEN_TILE
    return pl.pallas_call(
        _ffn_kernel,
        grid=(t // tm,),
        in_specs=[
            pl.BlockSpec((tm, d), lambda i: (i, 0)),
            pl.BlockSpec((None, 1, d), lambda i: (idx, 0, 0)),
            pl.BlockSpec((None, d, 2 * d_ff), lambda i: (idx, 0, 0), pipeline_mode=pl.Buffered(1)),
            pl.BlockSpec((None, d_ff, d), lambda i: (idx, 0, 0), pipeline_mode=pl.Buffered(1)),
        ],
        out_specs=pl.BlockSpec((tm, d), lambda i: (i, 0)),
        out_shape=jax.ShapeDtypeStruct((t, d), F32),
        scratch_shapes=[pltpu.VMEM((tm, d_ff), BF16)],
        compiler_params=_params("parallel"),
        name="ffn",
    )(x, norms, w_in, w_out)


def _head_sum(x, e_red):
    return jnp.dot(x.astype(BF16), e_red, preferred_element_type=F32)


def _spread(c, e_exp):
    hi = c.astype(BF16)
    lo = (c - hi.astype(F32)).astype(BF16)
    return jnp.dot(jnp.concatenate([hi, lo], axis=1), e_exp, preferred_element_type=F32)


def _rwkv_proj_kernel(*refs, has_vres):
    if has_vres:
        (x_ref, xp_ref, gn_ref, mu_ref, wrkv_ref, w0_ref, w1_ref, w2_ref, a0_ref, a1_ref,
         a2_ref, g1_ref, g2_ref, kk_ref, ka_ref, ered_ref, eexp_ref, v0_ref, v1_ref, v2_ref,
         vf_ref, r_o, ld_o, k_o, v_o, na_o, nb_o, g_o) = refs
    else:
        (x_ref, xp_ref, gn_ref, mu_ref, wrkv_ref, w0_ref, w1_ref, w2_ref, a0_ref, a1_ref,
         a2_ref, g1_ref, g2_ref, kk_ref, ka_ref, ered_ref, eexp_ref,
         r_o, ld_o, k_o, v_o, na_o, nb_o, g_o) = refs
    i = pl.program_id(0)
    gn = gn_ref[...]
    h = _rms(x_ref[...], gn)
    hp = _rms(xp_ref[7:8, :], gn) * jnp.where(i > 0, 1.0, 0.0)
    rows = lax.broadcasted_iota(jnp.int32, h.shape, 0)
    hs = jnp.where(rows == 0, hp, pltpu.roll(h, 1, axis=0))
    xx = hs - h
    mu = mu_ref[...]
    xr, xw, xk, xv, xa, xg = [(h + xx * mu[j:j + 1]).astype(BF16) for j in range(6)]
    w_dn = jnp.tanh(_mm(xw, w1_ref[...])).astype(BF16)
    a_dn = _mm(xa, a1_ref[...]).astype(BF16)
    g_dn = _sigmoid(_mm(xg, g1_ref[...])).astype(BF16)
    v_dn = _mm(xv, v1_ref[...]).astype(BF16) if has_vres else None
    n_col = PROJ_COL_TILE
    for c0 in range(0, h.shape[1], n_col):
        cols = slice(c0, c0 + n_col)
        mm = lambda a, w_ref: jnp.dot(a, w_ref[:, cols], preferred_element_type=F32)
        r = jnp.dot(xr, wrkv_ref[0, :, cols].astype(BF16), preferred_element_type=F32)
        k = jnp.dot(xk, wrkv_ref[1, :, cols].astype(BF16), preferred_element_type=F32)
        v = jnp.dot(xv, wrkv_ref[2, :, cols].astype(BF16), preferred_element_type=F32)
        w = w0_ref[:, cols] + mm(w_dn, w2_ref)
        z = -w
        wl = -(jnp.maximum(z, 0.0) + jnp.log(1.0 + jnp.exp(-jnp.abs(z)))) - 0.5
        ld_o[:, cols] = -jnp.exp(wl)
        a = _sigmoid(a0_ref[:, cols] + mm(a_dn, a2_ref))
        g_o[:, cols] = mm(g_dn, g2_ref).astype(BF16)
        if has_vres:
            v = v + (vf_ref[:, cols].astype(F32) - v) * _sigmoid(v0_ref[:, cols] + mm(v_dn, v2_ref))
        kkr = k * kk_ref[:, cols]
        norm = jnp.sqrt(_head_sum(kkr * kkr, ered_ref[cols, :]))
        kk = kkr * _spread(1.0 / jnp.maximum(norm, 1e-12), eexp_ref[:, cols])
        r_o[:, cols] = r.astype(BF16)
        k_o[:, cols] = (k * (1.0 + (a - 1.0) * ka_ref[:, cols])).astype(BF16)
        v_o[:, cols] = v.astype(BF16)
        na_o[:, cols] = (-kk).astype(BF16)
        nb_o[:, cols] = (kk * a).astype(BF16)


def _rwkv_proj(x, gn, mu, wrkv, layer, w0, w1, w2, a0, a1, a2, g1, g2, k_k, k_a, e_red, e_exp,
               vres):
    t, d = x.shape
    tm = TOKEN_TILE
    row = pl.BlockSpec((tm, d), lambda i: (i, 0))
    prev = pl.BlockSpec((8, d), lambda i: (jnp.maximum(i * (tm // 8) - 1, 0), 0))
    wrkv_spec = pl.BlockSpec((None,) + wrkv.shape[1:], lambda i: (layer, 0, 0, 0),
                             pipeline_mode=pl.Buffered(1))
    args = [x, x, gn, mu, wrkv, w0, w1, w2, a0, a1, a2, g1, g2, k_k, k_a, e_red, e_exp]
    specs = [row, prev, _full(gn.shape), _full(mu.shape), wrkv_spec, _full(w0.shape),
             _full(w1.shape), _full(w2.shape), _full(a0.shape), _full(a1.shape), _full(a2.shape),
             _full(g1.shape), _full(g2.shape), _full(k_k.shape), _full(k_a.shape),
             _full(e_red.shape), _full(e_exp.shape)]
    if vres is not None:
        v0, v1, v2, v_first = vres
        args += [v0, v1, v2, v_first]
        specs += [_full(v0.shape), _full(v1.shape), _full(v2.shape), row]
    return pl.pallas_call(
        functools.partial(_rwkv_proj_kernel, has_vres=vres is not None),
        grid=(t // tm,),
        in_specs=specs,
        out_specs=[row] * 7,
        out_shape=[jax.ShapeDtypeStruct((t, d), dt) for dt in (BF16, F32, BF16, BF16, BF16, BF16, BF16)],
        compiler_params=_params("parallel"),
        name="rwkv_proj",
    )(*args)


def _interleave(primary, filler):
    out = []
    for i in range(max(len(primary), len(filler))):
        if i < len(filler):
            out.append(filler[i])
        if i < len(primary):
            out.append(primary[i])
    return out


def _rwkv_scan_kernel(r_ref, ld_ref, k_ref, v_ref, na_ref, nb_ref, y_ref, s_ref):
    n_pairs = r_ref.shape[1] // LANES
    n_chunks = r_ref.shape[0] // CHUNK
    L = CHUNK
    assert 2 * L == LANES and L == HEAD_DIM and n_chunks % 2 == 0

    @pl.when(pl.program_id(0) == 0)
    def _():
        s_ref[...] = jnp.zeros_like(s_ref)

    ri = lax.broadcasted_iota(jnp.int32, (2 * L, 2 * L), 0)
    ci = lax.broadcasted_iota(jnp.int32, (2 * L, 2 * L), 1)
    same_head = (ri < L) == (ci < L)
    other_head = jnp.logical_not(same_head)
    strict = (ci & (L - 1)) < (ri & (L - 1))
    incl = (ci & (L - 1)) <= (ri & (L - 1))
    eye = (ri == ci).astype(F32)
    head0 = lax.broadcasted_iota(jnp.int32, (L, LANES), 1) < HEAD_DIM
    first_head = lax.broadcasted_iota(jnp.int32, (2 * L, LANES), 1) < HEAD_DIM
    tri = (lax.broadcasted_iota(jnp.int32, (L, L), 0)
           >= lax.broadcasted_iota(jnp.int32, (L, L), 1)).astype(BF16)
    zero = jnp.zeros((2 * L, LANES), BF16)
    pairs = range(n_pairs)
    lanes_of = lambda a, p: a[:, p * LANES:(p + 1) * LANES]
    stack2 = lambda a, b: jnp.concatenate([a, b], axis=0)

    state = [s_ref[p] for p in pairs]

    def prep_stages(chunks):
        items = [(c, p) for c in chunks for p in pairs]
        d = {}

        def elementwise():
            for c in chunks:
                rows = slice(c * L, (c + 1) * L)
                f32 = lambda ref: ref[rows, :].astype(F32)
                r, ld, k = f32(r_ref), ld_ref[rows, :], f32(k_ref)
                v, na, nb = f32(v_ref), f32(na_ref), f32(nb_ref)
                l1 = ld.astype(BF16)
                rem = ld - l1.astype(F32)
                l2 = rem.astype(BF16)
                l3 = (rem - l2.astype(F32)).astype(BF16)
                tri_dot = lambda a: jnp.dot(tri, a, preferred_element_type=F32)
                cum = tri_dot(l1) + (tri_dot(l2) + tri_dot(l3))
                cum_last = cum[L - 1:L, :]
                w_inv = jnp.exp(-cum)
                w_rel = jnp.exp(cum_last - cum)
                d["w_last", c] = jnp.exp(cum_last)
                d["x", c] = stack2(r * jnp.exp(cum), na * jnp.exp(cum - ld)).astype(BF16)
                d["bk", c] = stack2(nb * w_inv, k * w_inv).astype(BF16)
                d["kb", c] = stack2(k * w_inv, nb * w_inv).astype(BF16)
                d["keys", c] = stack2(nb * w_rel, k * w_rel).astype(BF16)
                d["vv", c] = stack2(v, v).astype(BF16)
                d["v", c] = v

        def gram():
            for c, p in items:
                rhs = stack2(jnp.where(first_head, lanes_of(d["bk", c], p), zero),
                             jnp.where(first_head, zero, lanes_of(d["kb", c], p)))
                d["g", c, p] = _mm_nt(lanes_of(d["x", c], p), rhs)

        def split():
            for c, p in items:
                g = d.pop(("g", c, p))
                g0, g1 = g[:, :LANES], g[:, LANES:]
                qs = stack2(g0[L:], g1[L:])
                ms = stack2(g0[:L], g1[:L])
                nil = jnp.where(same_head & strict, qs, 0.0)
                d["m_uv", c, p] = jnp.concatenate(
                    [jnp.where(same_head & incl, ms, 0.0),
                     jnp.where(other_head & incl, ms, 0.0)], axis=1).astype(BF16)
                d["akv", c, p] = _mm(jnp.where(other_head & strict, qs, 0.0),
                                     lanes_of(d["vv", c], p))
                d["t", c, p] = eye + nil
                d["pw", c, p] = _mm(nil, nil)

        def double():
            for c, p in items:
                pw = d["pw", c, p]
                t = d["t", c, p]
                both = _mm(pw, jnp.concatenate([t, pw], axis=1))
                d["t", c, p] = t + both[:, :LANES]
                d["pw", c, p] = both[:, LANES:]

        def finish():
            for c, p in items:
                t = d["t", c, p]
                d["t", c, p] = (t + _mm(d.pop(("pw", c, p)), t)).astype(BF16)

        return [elementwise, gram, split, double, double, double, double, finish], d

    def chunk_stages(c, d):
        loc = {}

        def project():
            for p in pairs:
                loc["xs", p] = _mm_nt(lanes_of(d["x", c], p), state[p])

        def solve():
            for p in pairs:
                xs = loc["xs", p]
                u = _mm(d["t", c, p], stack2(xs[L:], xs[L:]) + d["akv", c, p])
                loc["u", p] = jnp.where(head0, u[:L], u[L:])

        def emit():
            for p in pairs:
                xs, u = loc["xs", p], loc["u", p]
                ub = u.astype(BF16)
                vvp = lanes_of(d["vv", c], p)
                y = stack2(xs[:L], xs[:L]) + _mm(d["m_uv", c, p],
                                                 jnp.concatenate([ub, ub, vvp], axis=0))
                y_ref[c * L:(c + 1) * L, p * LANES:(p + 1) * LANES] = jnp.where(
                    head0, y[:L], y[L:]).astype(y_ref.dtype)
                vals = stack2(u, lanes_of(d["v", c], p))
                upd = _mm(vals.T, lanes_of(d["keys", c], p))
                state[p] = jnp.where(
                    same_head, state[p] * lanes_of(d["w_last", c], p) + upd, 0.0)

        return [project, solve, emit]

    groups = [(c, c + 1) for c in range(0, n_chunks, 2)]
    prep, data = prep_stages(groups[0])
    for stage in prep:
        stage()
    for gi, grp in enumerate(groups):
        chain = chunk_stages(grp[0], data) + chunk_stages(grp[1], data)
        if gi + 1 < len(groups):
            prep, nxt = prep_stages(groups[gi + 1])
        else:
            prep, nxt = [], None
        for stage in _interleave(chain, prep):
            stage()
        data = nxt
    for p in pairs:
        s_ref[p] = state[p]


def _rwkv_scan(r, ld, k, v, na, nb):
    t, d = r.shape
    blk = pl.BlockSpec((SCAN_BLOCK, d), lambda c: (c, 0))
    return pl.pallas_call(
        _rwkv_scan_kernel,
        grid=(t // SCAN_BLOCK,),
        in_specs=[blk] * 6,
        out_specs=blk,
        out_shape=jax.ShapeDtypeStruct((t, d), BF16),
        scratch_shapes=[pltpu.VMEM((d // LANES, LANES, LANES), F32)],
        compiler_params=_params("arbitrary"),
        name="rwkv_scan",
    )(r, ld, k, v, na, nb)


def _rwkv_post_kernel(y_ref, r_ref, k_ref, v_ref, g_ref, x_ref, lnw_ref, lnb_ref, rk_ref,
                      wo_ref, ered_ref, eexp_ref, o_ref):
    e_red = ered_ref[...]
    e_exp = eexp_ref[...]
    inv_n = 1.0 / HEAD_DIM
    n_sub = 4
    sub = y_ref.shape[0] // n_sub
    blocks = [slice(s * sub, (s + 1) * sub) for s in range(n_sub)]
    ys = [y_ref[b, :].astype(F32) for b in blocks]
    sum_y = [_head_sum(y, e_red) for y in ys]
    sum_b = [_head_sum(r_ref[b, :].astype(F32) * k_ref[b, :].astype(F32) * rk_ref[...], e_red)
             for b in blocks]
    yc = [y - _spread(s * inv_n, e_exp) for y, s in zip(ys, sum_y)]
    bonus = [_spread(s, e_exp) * v_ref[b, :].astype(F32) for s, b in zip(sum_b, blocks)]
    var = [_head_sum(c * c, e_red) * inv_n for c in yc]
    rstd = [_spread(lax.rsqrt(v + GN_EPS), e_exp) for v in var]
    z = [((c * rs * lnw_ref[...] + lnb_ref[...]) + bn) * g_ref[b, :].astype(F32)
         for c, rs, bn, b in zip(yc, rstd, bonus, blocks)]
    o_ref[...] = x_ref[...] + _mm(jnp.concatenate(z, axis=0), wo_ref[...])


def _rwkv_post(y, r, k, v, g, x, ln_w, ln_b, r_k, w_o, e_red, e_exp):
    t, d = x.shape
    tm = TOKEN_TILE
    row = pl.BlockSpec((tm, d), lambda i: (i, 0))
    return pl.pallas_call(
        _rwkv_post_kernel,
        grid=(t // tm,),
        in_specs=[row] * 6 + [_full(ln_w.shape), _full(ln_b.shape), _full(r_k.shape),
                              _resident(w_o.shape), _full(e_red.shape), _full(e_exp.shape)],
        out_specs=row,
        out_shape=jax.ShapeDtypeStruct((t, d), F32),
        compiler_params=_params("parallel"),
        name="rwkv_post",
    )(y, r, k, v, g, x, ln_w, ln_b, r_k, w_o, e_red, e_exp)


def _store_dilated(dst_ref, y, stage_ref, dil):
    rows, d = y.shape
    if dil == 1:
        dst_ref[...] = y.astype(dst_ref.dtype)
        return
    n_tiles = d // LANES
    for c in range(n_tiles):
        stage_ref[c] = y[:, c * LANES:(c + 1) * LANES]
    for r in range(dil):
        for c in range(n_tiles):
            lo = r * d + c * LANES
            dst_ref[:, lo:lo + LANES] = (
                stage_ref[c, pl.ds(r, rows // dil, stride=dil), :].astype(dst_ref.dtype))


def _proj_blocks(hn, w_ref, n_normed, gain_ref, e_red, e_exp, emit):
    d = hn.shape[1]
    n_blk = w_ref.shape[1] // d
    normed = sorted(range(n_normed), reverse=True)
    order = normed + sorted(range(n_normed, n_blk), reverse=True)
    ys, ms = {}, {}

    def project(b):
        ys[b] = jnp.dot(hn, w_ref[:, b * d:(b + 1) * d].astype(BF16),
                        preferred_element_type=F32)

    def head_sums(b):
        if b in normed:
            ms[b] = _head_sum(ys[b] * ys[b], e_red) * (1.0 / HEAD_DIM)

    def finish(b):
        y = ys.pop(b)
        if b in normed:
            y = y * _spread(lax.rsqrt(ms.pop(b) + RMS_EPS), e_exp) * gain_ref[b]
        emit(b, y)

    for i in range(n_blk + 2):
        for lag, stage in enumerate((project, head_sums, finish)):
            if 0 <= i - lag < n_blk:
                stage(order[i - lag])


def _kv_proj_kernel(x_ref, gn_ref, w_ref, gain_ref, ered_ref, eexp_ref, o0_ref, o1_ref, o2_ref,
                    *stage_refs):
    n_groups = len(DIL_GROUPS)
    out_refs = (o0_ref, o1_ref, o2_ref)
    hn = _rms(x_ref[...], gn_ref[...]).astype(BF16)

    def emit(b, y):
        is_value, grp = divmod(b, n_groups)
        _store_dilated(out_refs[grp].at[is_value], y, stage_refs[grp], DIL_GROUPS[grp][1])

    _proj_blocks(hn, w_ref, n_groups, gain_ref, ered_ref[...], eexp_ref[...], emit)


def _kv_proj(x, gn, w_kv, gains, e_red, e_exp):
    t, d = x.shape
    tm = TOKEN_TILE
    return pl.pallas_call(
        _kv_proj_kernel,
        grid=(t // tm,),
        in_specs=[pl.BlockSpec((tm, d), lambda i: (i, 0)), _full(gn.shape),
                  _resident(w_kv.shape), _full(gains.shape), _full(e_red.shape),
                  _full(e_exp.shape)],
        out_specs=[pl.BlockSpec((2, tm // dil, dil * d), lambda i: (0, i, 0))
                   for _, dil in DIL_GROUPS],
        out_shape=[jax.ShapeDtypeStruct((2, t // dil, dil * d), BF16) for _, dil in DIL_GROUPS],
        scratch_shapes=[pltpu.VMEM((d // LANES, tm, LANES), F32) for _ in DIL_GROUPS],
        compiler_params=_params("parallel"),
        name="kv_proj",
    )(x, gn, w_kv, gains, e_red, e_exp)


def _q_proj_kernel(x_ref, gn_ref, w_ref, gain_ref, ered_ref, eexp_ref, o0_ref, o1_ref, o2_ref,
                   *stage_refs):
    out_refs = (o0_ref, o1_ref, o2_ref)
    hn = _rms(x_ref[...], gn_ref[...]).astype(BF16)
    scale = HEAD_DIM ** -0.5 * LOG2_E

    def emit(grp, y):
        _store_dilated(out_refs[grp], y * scale, stage_refs[grp], DIL_GROUPS[grp][1])

    _proj_blocks(hn, w_ref, len(DIL_GROUPS), gain_ref, ered_ref[...], eexp_ref[...], emit)


def _q_proj(x, gn, w_q, layer, gains, e_red, e_exp):
    t, d = x.shape
    tm = TOKEN_TILE
    return pl.pallas_call(
        _q_proj_kernel,
        grid=(t // tm,),
        in_specs=[pl.BlockSpec((tm, d), lambda i: (i, 0)), _full(gn.shape),
                  pl.BlockSpec((None,) + w_q.shape[1:], lambda i: (layer, 0, 0),
                               pipeline_mode=pl.Buffered(1)),
                  _full(gains.shape), _full(e_red.shape), _full(e_exp.shape)],
        out_specs=[pl.BlockSpec((tm // dil, dil * d), lambda i: (i, 0)) for _, dil in DIL_GROUPS],
        out_shape=[jax.ShapeDtypeStruct((t // dil, dil * d), BF16) for _, dil in DIL_GROUPS],
        scratch_shapes=[pltpu.VMEM((d // LANES, tm, LANES), F32) for _ in DIL_GROUPS],
        compiler_params=_params("parallel"),
        name="q_proj",
    )(x, gn, w_q, gains, e_red, e_exp)


def _dil_attn_kernel(q_ref, kp_ref, kc_ref, vp_ref, vc_ref, o_ref, m_ref, l_ref):
    n = pl.program_id(1)
    blk = ATT_BLK
    n_pairs = q_ref.shape[1] // LANES
    n_sub = q_ref.shape[0] // blk
    qi = lax.broadcasted_iota(jnp.int32, (blk, 2 * blk), 0)
    kj = lax.broadcasted_iota(jnp.int32, (blk, 2 * blk), 1)
    band = (kj >= qi) & (kj <= qi + blk)
    valid = [band & ((n > 0) | (kj >= blk))] + [band] * (n_sub - 1)
    lane = lax.broadcasted_iota(jnp.int32, (blk, LANES), 1)
    head0 = lane < HEAD_DIM
    dn_nt = (((1,), (1,)), ((), ()))
    zero = jnp.zeros((blk, LANES), BF16)
    lanes_of = lambda a, p: a[:, p * LANES:(p + 1) * LANES]

    def keys_of(prev_ref, cur_ref, s, p):
        if s == 0:
            return jnp.concatenate([lanes_of(prev_ref, p), lanes_of(cur_ref, p)[:blk]], axis=0)
        return lanes_of(cur_ref, p)[(s - 1) * blk:(s + 1) * blk]

    waves = [(s, range(p0, p0 + ATT_WAVE)) for s in range(n_sub)
             for p0 in range(0, n_pairs, ATT_WAVE)]
    scores, probs = {}, {}
    m_acc = [jnp.zeros((blk, LANES), F32) for _ in range(n_sub)]
    l_acc = [jnp.ones((blk, LANES), F32) for _ in range(n_sub)]

    def score_stage(s, ps):
        for p in ps:
            qp = lanes_of(q_ref, p)[s * blk:(s + 1) * blk]
            q2 = jnp.concatenate([jnp.where(head0, qp, zero), jnp.where(head0, zero, qp)], axis=0)
            scores[s, p] = lax.dot_general(q2, keys_of(kp_ref, kc_ref, s, p), dn_nt,
                                           preferred_element_type=F32)

    def softmax_stage(s, ps):
        for p in ps:
            sc2 = scores.pop((s, p))
            halves = []
            for hh in range(2):
                sc = jnp.where(valid[s], sc2[hh * blk:(hh + 1) * blk], NEG_INF)
                mx = jnp.max(sc, axis=-1, keepdims=True)
                pe = jnp.exp2(sc - mx)
                hit = lane == 2 * p + hh
                m_acc[s] = jnp.where(hit, mx, m_acc[s])
                l_acc[s] = jnp.where(hit, jnp.sum(pe, axis=-1, keepdims=True), l_acc[s])
                halves.append(pe.astype(BF16))
            probs[s, p] = jnp.concatenate(halves, axis=0)

    def output_stage(s, ps):
        for p in ps:
            o2 = jnp.dot(probs.pop((s, p)), keys_of(vp_ref, vc_ref, s, p),
                         preferred_element_type=F32)
            o_ref[s * blk:(s + 1) * blk, p * LANES:(p + 1) * LANES] = jnp.where(
                head0, o2[:blk], o2[blk:]).astype(o_ref.dtype)

    for i in range(len(waves) + 2):
        for lag, stage in enumerate((score_stage, softmax_stage, output_stage)):
            if 0 <= i - lag < len(waves):
                stage(*waves[i - lag])
    for s in range(n_sub):
        m_ref[s * blk:(s + 1) * blk, :] = m_acc[s]
        l_ref[s * blk:(s + 1) * blk, :] = l_acc[s]


def _dil_attn(q, kv, dil):
    m_rows, wide = q.shape
    d = wide // dil
    blk = ATT_BLK
    step = ATT_SUB * blk
    cur = lambda a: pl.BlockSpec((None, step, d), lambda r, n: (a, n, r))
    prev = lambda a: pl.BlockSpec((None, blk, d),
                                  lambda r, n: (a, jnp.maximum(ATT_SUB * n - 1, 0), r))
    return pl.pallas_call(
        _dil_attn_kernel,
        grid=(dil, m_rows // step),
        in_specs=[pl.BlockSpec((step, d), lambda r, n: (n, r)), prev(0), cur(0), prev(1), cur(1)],
        out_specs=[pl.BlockSpec((step, d), lambda r, n: (n, r)),
                   pl.BlockSpec((step, LANES), lambda r, n: (n, r)),
                   pl.BlockSpec((step, LANES), lambda r, n: (n, r))],
        out_shape=[jax.ShapeDtypeStruct((m_rows, dil * d), BF16),
                   jax.ShapeDtypeStruct((m_rows, dil * LANES), F32),
                   jax.ShapeDtypeStruct((m_rows, dil * LANES), F32)],
        compiler_params=_params("parallel", "parallel"),
        name="dil_attn",
    )(q, kv, kv, kv, kv)


def _load_dilated(src_ref, stage_ref, dil):
    if dil == 1:
        return src_ref[...]
    sub = src_ref.shape[0]
    n_tiles = src_ref.shape[1] // dil // LANES
    for r in range(dil):
        for c in range(n_tiles):
            lo = (r * n_tiles + c) * LANES
            stage_ref[c, pl.ds(r, sub, stride=dil), :] = src_ref[:, lo:lo + LANES].astype(F32)
    return jnp.concatenate([stage_ref[c] for c in range(n_tiles)], axis=1)


def _attn_combine_kernel(o0, o1, o2, m0, m1, m2, l0, l1, l2, x_ref, wo_ref, eexp_ref, out_ref,
                         os1, os2, ms1, ms2, ls1, ls2):
    dils = [dil for _, dil in DIL_GROUPS]
    ms = [_load_dilated(ref, st, dil) for ref, st, dil in zip((m0, m1, m2), (None, ms1, ms2), dils)]
    ls = [_load_dilated(ref, st, dil) for ref, st, dil in zip((l0, l1, l2), (None, ls1, ls2), dils)]
    top = jnp.maximum(jnp.maximum(ms[0], ms[1]), ms[2])
    es = [jnp.exp2(mg - top) for mg in ms]
    den = es[0] * ls[0] + es[1] * ls[1] + es[2] * ls[2]
    e_exp = eexp_ref[...]
    acc = None
    for eg, ref, st, dil in zip(es, (o0, o1, o2), (None, os1, os2), dils):
        term = _spread(eg / den, e_exp) * _load_dilated(ref, st, dil)
        acc = term if acc is None else acc + term
    out_ref[...] = x_ref[...] + _mm(acc, wo_ref[...])


def _attn_combine(parts, x, w_o, e_exp):
    t, d = x.shape
    tm = TOKEN_TILE
    row = pl.BlockSpec((tm, d), lambda i: (i, 0))
    dils = [dil for _, dil in DIL_GROUPS]
    o_specs = [pl.BlockSpec((tm // dil, dil * d), lambda i: (i, 0)) for dil in dils]
    s_specs = [pl.BlockSpec((tm // dil, dil * LANES), lambda i: (i, 0)) for dil in dils]
    os_, ms_, ls_ = zip(*parts)
    return pl.pallas_call(
        _attn_combine_kernel,
        grid=(t // tm,),
        in_specs=o_specs + s_specs + s_specs + [row, _resident(w_o.shape), _full(e_exp.shape)],
        out_specs=row,
        out_shape=jax.ShapeDtypeStruct((t, d), F32),
        scratch_shapes=([pltpu.VMEM((d // LANES, tm, LANES), F32)] * 2
                        + [pltpu.VMEM((1, tm, LANES), F32)] * 4),
        compiler_params=_params("parallel"),
        name="attn_combine",
    )(*os_, *ms_, *ls_, x, w_o, e_exp)


def kernel(x, ffn_norm, ffn_w_in, ffn_w_out, mix_norm, rwkv_mu, rwkv_w_rkv, rwkv_w0, rwkv_w1, rwkv_w2, rwkv_a0, rwkv_a1, rwkv_a2, rwkv_v0, rwkv_v1, rwkv_v2, rwkv_g1, rwkv_g2, rwkv_k_k, rwkv_k_a, rwkv_r_k, rwkv_ln_w, rwkv_ln_b, rwkv_w_o, kv_norm, w_kv, k_norm, attn_w_q, q_norm, attn_w_o):
    bsz, t, d = x.shape
    assert bsz == 1 and d % LANES == 0 and t % max(w for w, _ in DIL_GROUPS) == 0
    assert all(w // dil == ATT_BLK for w, dil in DIL_GROUPS)
    depth = ffn_norm.shape[0]
    n_a = rwkv_mu.shape[0]
    n_groups = len(DIL_GROUPS)
    row = lambda p: p.reshape(1, -1)

    ffn_g = ffn_norm.reshape(2 * depth, 1, d)
    ffn_wi = ffn_w_in.reshape(2 * depth, d, -1)
    ffn_wo = ffn_w_out.reshape(2 * depth, -1, d)
    head_of_lane = jnp.arange(d) // HEAD_DIM
    one_hot = (jnp.arange(LANES)[:, None] == head_of_lane[None, :]).astype(BF16)
    e_red = one_hot.T
    e_exp = jnp.concatenate([one_hot, one_hot], axis=0)

    xs = x.reshape(t, d)
    v_first = None
    kvs = None
    for l in range(depth):
        xs = _ffn(xs, ffn_g, ffn_wi, ffn_wo, 2 * l)
        if l < n_a:
            vres = None
            if l > 0:
                vres = (row(rwkv_v0[l - 1]), rwkv_v1[l - 1].astype(BF16),
                        rwkv_v2[l - 1].astype(BF16), v_first)
            r, ld, k, v, na, nb, g = _rwkv_proj(
                xs, row(mix_norm[l]), rwkv_mu[l], rwkv_w_rkv, l, row(rwkv_w0[l]),
                rwkv_w1[l].astype(BF16), rwkv_w2[l].astype(BF16), row(rwkv_a0[l]),
                rwkv_a1[l].astype(BF16), rwkv_a2[l].astype(BF16), rwkv_g1[l].astype(BF16),
                rwkv_g2[l].astype(BF16), row(rwkv_k_k[l]), row(rwkv_k_a[l]), e_red, e_exp, vres)
            if l == 0:
                v_first = v
            y = _rwkv_scan(r, ld, k, v, na, nb)
            xs = _rwkv_post(y, r, k, v, g, xs, row(rwkv_ln_w[l]), row(rwkv_ln_b[l]),
                            row(rwkv_r_k[l]), rwkv_w_o[l].astype(BF16), e_red, e_exp)
        else:
            i = l - n_a
            q_gain = jnp.tile(q_norm[i], (1, d // HEAD_DIM)).reshape(n_groups, 1, d)
            qs = _q_proj(xs, row(mix_norm[l]), attn_w_q, i, q_gain, e_red, e_exp)
            parts = [_dil_attn(qs[grp], kvs[grp], DIL_GROUPS[grp][1]) for grp in range(n_groups)]
            xs = _attn_combine(parts, xs, attn_w_o[i].astype(BF16), e_exp)
        xs = _ffn(xs, ffn_g, ffn_wi, ffn_wo, 2 * l + 1)
        if l == n_a - 1:
            k_gain = jnp.tile(k_norm, (1, d // HEAD_DIM)).reshape(n_groups, 1, d)
            kvs = _kv_proj(xs, row(kv_norm), w_kv.astype(BF16), k_gain, e_red, e_exp)
    return xs.reshape(bsz, t, d)
```

```python
import functools

import jax
import jax.numpy as jnp
from jax import lax
from jax.experimental import pallas as pl
from jax.experimental.pallas import tpu as pltpu

F32 = jnp.float32
BF16 = jnp.bfloat16

RMS_EPS = 1e-6
GN_EPS = 64e-5
HEAD_DIM = 64
LANES = 128
DIL_GROUPS = ((128, 1), (512, 4), (2048, 16))
ATT_BLK = 128
ATT_SUB = 8
ATT_WAVE = 1
NEG_INF = -1e30
LOG2_E = 1.4426950408889634
CHUNK = 64
SCAN_BLOCK = 512
TOKEN_TILE = 512
FFN_COL_TILE = 256
PROJ_COL_TILE = 256
VMEM_LIMIT = 56 * 1024 * 1024


def _params(*sem):
    return pltpu.CompilerParams(dimension_semantics=sem, vmem_limit_bytes=VMEM_LIMIT)


def _mm(a, b):
    return jnp.dot(a.astype(BF16), b.astype(BF16), preferred_element_type=F32)


def _mm_nt(a, b):
    return lax.dot_general(a.astype(BF16), b.astype(BF16), (((1,), (1,)), ((), ())),
                           preferred_element_type=F32)


def _rms(x, g):
    return x * lax.rsqrt(jnp.mean(x * x, axis=-1, keepdims=True) + RMS_EPS) * g


def _sigmoid(x):
    return 1.0 / (1.0 + jnp.exp(-x))


def _full(shape):
    n = len(shape)
    return pl.BlockSpec(shape, lambda *_: (0,) * n)


def _resident(shape):
    n = len(shape)
    return pl.BlockSpec(shape, lambda *_: (0,) * n, pipeline_mode=pl.Buffered(1))


def _ffn_kernel(x_ref, g_ref, win_ref, wout_ref, o_ref, act_ref):
    x = x_ref[...]
    hn = _rms(x, g_ref[...]).astype(BF16)
    d_ff = wout_ref.shape[0]
    for c in range(d_ff // FFN_COL_TILE):
        lo = c * FFN_COL_TILE
        gate = jnp.dot(hn, win_ref[:, lo:lo + FFN_COL_TILE].astype(BF16),
                       preferred_element_type=F32)
        up = jnp.dot(hn, win_ref[:, d_ff + lo:d_ff + lo + FFN_COL_TILE].astype(BF16),
                     preferred_element_type=F32)
        act_ref[:, lo:lo + FFN_COL_TILE] = (gate * _sigmoid(gate) * up).astype(BF16)
    o_ref[...] = x + 0.5 * jnp.dot(act_ref[...], wout_ref[...].astype(BF16),
                                   preferred_element_type=F32)


def _ffn(x, norms, w_in, w_out, idx):
    t, d = x.shape
    d_ff = w_out.shape[1]
    tm = TOKEN_TILE
    return pl.pallas_call(
        _ffn_kernel,
        grid=(t // tm,),
        in_specs=[
            pl.BlockSpec((tm, d), lambda i: (i, 0)),
            pl.BlockSpec((None, 1, d), lambda i: (idx, 0, 0)),
            pl.BlockSpec((None, d, 2 * d_ff), lambda i: (idx, 0, 0), pipeline_mode=pl.Buffered(1)),
            pl.BlockSpec((None, d_ff, d), lambda i: (idx, 0, 0), pipeline_mode=pl.Buffered(1)),
        ],
        out_specs=pl.BlockSpec((tm, d), lambda i: (i, 0)),
        out_shape=jax.ShapeDtypeStruct((t, d), F32),
        scratch_shapes=[pltpu.VMEM((tm, d_ff), BF16)],
        compiler_params=_params("parallel"),
        name="ffn",
    )(x, norms, w_in, w_out)


def _head_sum(x, e_red):
    return jnp.dot(x.astype(BF16), e_red, preferred_element_type=F32)


def _spread(c, e_exp):
    hi = c.astype(BF16)
    lo = (c - hi.astype(F32)).astype(BF16)
    return jnp.dot(jnp.concatenate([hi, lo], axis=1), e_exp, preferred_element_type=F32)


def _rwkv_proj_kernel(*refs, has_vres):
    if has_vres:
        (x_ref, xp_ref, gn_ref, mu_ref, wrkv_ref, w0_ref, w1_ref, w2_ref, a0_ref, a1_ref,
         a2_ref, g1_ref, g2_ref, kk_ref, ka_ref, ered_ref, eexp_ref, v0_ref, v1_ref, v2_ref,
         vf_ref, r_o, ld_o, k_o, v_o, na_o, nb_o, g_o) = refs
    else:
        (x_ref, xp_ref, gn_ref, mu_ref, wrkv_ref, w0_ref, w1_ref, w2_ref, a0_ref, a1_ref,
         a2_ref, g1_ref, g2_ref, kk_ref, ka_ref, ered_ref, eexp_ref,
         r_o, ld_o, k_o, v_o, na_o, nb_o, g_o) = refs
    i = pl.program_id(0)
    gn = gn_ref[...]
    h = _rms(x_ref[...], gn)
    hp = _rms(xp_ref[7:8, :], gn) * jnp.where(i > 0, 1.0, 0.0)
    rows = lax.broadcasted_iota(jnp.int32, h.shape, 0)
    hs = jnp.where(rows == 0, hp, pltpu.roll(h, 1, axis=0))
    xx = hs - h
    mu = mu_ref[...]
    xr, xw, xk, xv, xa, xg = [(h + xx * mu[j:j + 1]).astype(BF16) for j in range(6)]
    w_dn = jnp.tanh(_mm(xw, w1_ref[...])).astype(BF16)
    a_dn = _mm(xa, a1_ref[...]).astype(BF16)
    g_dn = _sigmoid(_mm(xg, g1_ref[...])).astype(BF16)
    v_dn = _mm(xv, v1_ref[...]).astype(BF16) if has_vres else None
    n_col = PROJ_COL_TILE
    for c0 in range(0, h.shape[1], n_col):
        cols = slice(c0, c0 + n_col)
        mm = lambda a, w_ref: jnp.dot(a, w_ref[:, cols], preferred_element_type=F32)
        r = jnp.dot(xr, wrkv_ref[0, :, cols].astype(BF16), preferred_element_type=F32)
        k = jnp.dot(xk, wrkv_ref[1, :, cols].astype(BF16), preferred_element_type=F32)
        v = jnp.dot(xv, wrkv_ref[2, :, cols].astype(BF16), preferred_element_type=F32)
        w = w0_ref[:, cols] + mm(w_dn, w2_ref)
        z = -w
        wl = -(jnp.maximum(z, 0.0) + jnp.log(1.0 + jnp.exp(-jnp.abs(z)))) - 0.5
        ld_o[:, cols] = -jnp.exp(wl)
        a = _sigmoid(a0_ref[:, cols] + mm(a_dn, a2_ref))
        g_o[:, cols] = mm(g_dn, g2_ref).astype(BF16)
        if has_vres:
            v = v + (vf_ref[:, cols].astype(F32) - v) * _sigmoid(v0_ref[:, cols] + mm(v_dn, v2_ref))
        kkr = k * kk_ref[:, cols]
        norm = jnp.sqrt(_head_sum(kkr * kkr, ered_ref[cols, :]))
        kk = kkr * _spread(1.0 / jnp.maximum(norm, 1e-12), eexp_ref[:, cols])
        r_o[:, cols] = r.astype(BF16)
        k_o[:, cols] = (k * (1.0 + (a - 1.0) * ka_ref[:, cols])).astype(BF16)
        v_o[:, cols] = v.astype(BF16)
        na_o[:, cols] = (-kk).astype(BF16)
        nb_o[:, cols] = (kk * a).astype(BF16)


def _rwkv_proj(x, gn, mu, wrkv, layer, w0, w1, w2, a0, a1, a2, g1, g2, k_k, k_a, e_red, e_exp,
               vres):
    t, d = x.shape
    tm = TOKEN_TILE
    row = pl.BlockSpec((tm, d), lambda i: (i, 0))
    prev = pl.BlockSpec((8, d), lambda i: (jnp.maximum(i * (tm // 8) - 1, 0), 0))
    wrkv_spec = pl.BlockSpec((None,) + wrkv.shape[1:], lambda i: (layer, 0, 0, 0),
                             pipeline_mode=pl.Buffered(1))
    args = [x, x, gn, mu, wrkv, w0, w1, w2, a0, a1, a2, g1, g2, k_k, k_a, e_red, e_exp]
    specs = [row, prev, _full(gn.shape), _full(mu.shape), wrkv_spec, _full(w0.shape),
             _full(w1.shape), _full(w2.shape), _full(a0.shape), _full(a1.shape), _full(a2.shape),
             _full(g1.shape), _full(g2.shape), _full(k_k.shape), _full(k_a.shape),
             _full(e_red.shape), _full(e_exp.shape)]
    if vres is not None:
        v0, v1, v2, v_first = vres
        args += [v0, v1, v2, v_first]
        specs += [_full(v0.shape), _full(v1.shape), _full(v2.shape), row]
    return pl.pallas_call(
        functools.partial(_rwkv_proj_kernel, has_vres=vres is not None),
        grid=(t // tm,),
        in_specs=specs,
        out_specs=[row] * 7,
        out_shape=[jax.ShapeDtypeStruct((t, d), dt) for dt in (BF16, F32, BF16, BF16, BF16, BF16, BF16)],
        compiler_params=_params("parallel"),
        name="rwkv_proj",
    )(*args)


def _interleave(primary, filler):
    out = []
    for i in range(max(len(primary), len(filler))):
        if i < len(filler):
            out.append(filler[i])
        if i < len(primary):
            out.append(primary[i])
    return out


def _rwkv_scan_kernel(r_ref, ld_ref, k_ref, v_ref, na_ref, nb_ref, y_ref, s_ref):
    n_pairs = r_ref.shape[1] // LANES
    n_chunks = r_ref.shape[0] // CHUNK
    L = CHUNK
    assert 2 * L == LANES and L == HEAD_DIM and n_chunks % 2 == 0

    @pl.when(pl.program_id(0) == 0)
    def _():
        s_ref[...] = jnp.zeros_like(s_ref)

    ri = lax.broadcasted_iota(jnp.int32, (2 * L, 2 * L), 0)
    ci = lax.broadcasted_iota(jnp.int32, (2 * L, 2 * L), 1)
    same_head = (ri < L) == (ci < L)
    other_head = jnp.logical_not(same_head)
    strict = (ci & (L - 1)) < (ri & (L - 1))
    incl = (ci & (L - 1)) <= (ri & (L - 1))
    eye = (ri == ci).astype(F32)
    head0 = lax.broadcasted_iota(jnp.int32, (L, LANES), 1) < HEAD_DIM
    first_head = lax.broadcasted_iota(jnp.int32, (2 * L, LANES), 1) < HEAD_DIM
    tri = (lax.broadcasted_iota(jnp.int32, (L, L), 0)
           >= lax.broadcasted_iota(jnp.int32, (L, L), 1)).astype(BF16)
    zero = jnp.zeros((2 * L, LANES), BF16)
    pairs = range(n_pairs)
    lanes_of = lambda a, p: a[:, p * LANES:(p + 1) * LANES]
    stack2 = lambda a, b: jnp.concatenate([a, b], axis=0)

    state = [s_ref[p] for p in pairs]

    def prep_stages(chunks):
        items = [(c, p) for c in chunks for p in pairs]
        d = {}

        def elementwise():
            for c in chunks:
                rows = slice(c * L, (c + 1) * L)
                f32 = lambda ref: ref[rows, :].astype(F32)
                r, ld, k = f32(r_ref), ld_ref[rows, :], f32(k_ref)
                v, na, nb = f32(v_ref), f32(na_ref), f32(nb_ref)
                l1 = ld.astype(BF16)
                rem = ld - l1.astype(F32)
                l2 = rem.astype(BF16)
                l3 = (rem - l2.astype(F32)).astype(BF16)
                tri_dot = lambda a: jnp.dot(tri, a, preferred_element_type=F32)
                cum = tri_dot(l1) + (tri_dot(l2) + tri_dot(l3))
                cum_last = cum[L - 1:L, :]
                w_inv = jnp.exp(-cum)
                w_rel = jnp.exp(cum_last - cum)
                d["w_last", c] = jnp.exp(cum_last)
                d["x", c] = stack2(r * jnp.exp(cum), na * jnp.exp(cum - ld)).astype(BF16)
                d["bk", c] = stack2(nb * w_inv, k * w_inv).astype(BF16)
                d["kb", c] = stack2(k * w_inv, nb * w_inv).astype(BF16)
                d["keys", c] = stack2(nb * w_rel, k * w_rel).astype(BF16)
                d["vv", c] = stack2(v, v).astype(BF16)
                d["v", c] = v

        def gram():
            for c, p in items:
                rhs = stack2(jnp.where(first_head, lanes_of(d["bk", c], p), zero),
                             jnp.where(first_head, zero, lanes_of(d["kb", c], p)))
                d["g", c, p] = _mm_nt(lanes_of(d["x", c], p), rhs)

        def split():
            for c, p in items:
                g = d.pop(("g", c, p))
                g0, g1 = g[:, :LANES], g[:, LANES:]
                qs = stack2(g0[L:], g1[L:])
                ms = stack2(g0[:L], g1[:L])
                nil = jnp.where(same_head & strict, qs, 0.0)
                d["m_uv", c, p] = jnp.concatenate(
                    [jnp.where(same_head & incl, ms, 0.0),
                     jnp.where(other_head & incl, ms, 0.0)], axis=1).astype(BF16)
                d["akv", c, p] = _mm(jnp.where(other_head & strict, qs, 0.0),
                                     lanes_of(d["vv", c], p))
                d["t", c, p] = eye + nil
                d["pw", c, p] = _mm(nil, nil)

        def double():
            for c, p in items:
                pw = d["pw", c, p]
                t = d["t", c, p]
                both = _mm(pw, jnp.concatenate([t, pw], axis=1))
                d["t", c, p] = t + both[:, :LANES]
                d["pw", c, p] = both[:, LANES:]

        def finish():
            for c, p in items:
                t = d["t", c, p]
                d["t", c, p] = (t + _mm(d.pop(("pw", c, p)), t)).astype(BF16)

        return [elementwise, gram, split, double, double, double, double, finish], d

    def chunk_stages(c, d):
        loc = {}

        def project():
            for p in pairs:
                loc["xs", p] = _mm_nt(lanes_of(d["x", c], p), state[p])

        def solve():
            for p in pairs:
                xs = loc["xs", p]
                u = _mm(d["t", c, p], stack2(xs[L:], xs[L:]) + d["akv", c, p])
                loc["u", p] = jnp.where(head0, u[:L], u[L:])

        def emit():
            for p in pairs:
                xs, u = loc["xs", p], loc["u", p]
                ub = u.astype(BF16)
                vvp = lanes_of(d["vv", c], p)
                y = stack2(xs[:L], xs[:L]) + _mm(d["m_uv", c, p],
                                                 jnp.concatenate([ub, ub, vvp], axis=0))
                y_ref[c * L:(c + 1) * L, p * LANES:(p + 1) * LANES] = jnp.where(
                    head0, y[:L], y[L:]).astype(y_ref.dtype)
                vals = stack2(u, lanes_of(d["v", c], p))
                upd = _mm(vals.T, lanes_of(d["keys", c], p))
                state[p] = jnp.where(
                    same_head, state[p] * lanes_of(d["w_last", c], p) + upd, 0.0)

        return [project, solve, emit]

    groups = [(c, c + 1) for c in range(0, n_chunks, 2)]
    prep, data = prep_stages(groups[0])
    for stage in prep:
        stage()
    for gi, grp in enumerate(groups):
        chain = chunk_stages(grp[0], data) + chunk_stages(grp[1], data)
        if gi + 1 < len(groups):
            prep, nxt = prep_stages(groups[gi + 1])
        else:
            prep, nxt = [], None
        for stage in _interleave(chain, prep):
            stage()
        data = nxt
    for p in pairs:
        s_ref[p] = state[p]


def _rwkv_scan(r, ld, k, v, na, nb):
    t, d = r.shape
    blk = pl.BlockSpec((SCAN_BLOCK, d), lambda c: (c, 0))
    return pl.pallas_call(
        _rwkv_scan_kernel,
        grid=(t // SCAN_BLOCK,),
        in_specs=[blk] * 6,
        out_specs=blk,
        out_shape=jax.ShapeDtypeStruct((t, d), BF16),
        scratch_shapes=[pltpu.VMEM((d // LANES, LANES, LANES), F32)],
        compiler_params=_params("arbitrary"),
        name="rwkv_scan",
    )(r, ld, k, v, na, nb)


def _rwkv_post_kernel(y_ref, r_ref, k_ref, v_ref, g_ref, x_ref, lnw_ref, lnb_ref, rk_ref,
                      wo_ref, ered_ref, eexp_ref, o_ref):
    e_red = ered_ref[...]
    e_exp = eexp_ref[...]
    inv_n = 1.0 / HEAD_DIM
    n_sub = 4
    sub = y_ref.shape[0] // n_sub
    blocks = [slice(s * sub, (s + 1) * sub) for s in range(n_sub)]
    ys = [y_ref[b, :].astype(F32) for b in blocks]
    sum_y = [_head_sum(y, e_red) for y in ys]
    sum_b = [_head_sum(r_ref[b, :].astype(F32) * k_ref[b, :].astype(F32) * rk_ref[...], e_red)
             for b in blocks]
    yc = [y - _spread(s * inv_n, e_exp) for y, s in zip(ys, sum_y)]
    bonus = [_spread(s, e_exp) * v_ref[b, :].astype(F32) for s, b in zip(sum_b, blocks)]
    var = [_head_sum(c * c, e_red) * inv_n for c in yc]
    rstd = [_spread(lax.rsqrt(v + GN_EPS), e_exp) for v in var]
    z = [((c * rs * lnw_ref[...] + lnb_ref[...]) + bn) * g_ref[b, :].astype(F32)
         for c, rs, bn, b in zip(yc, rstd, bonus, blocks)]
    o_ref[...] = x_ref[...] + _mm(jnp.concatenate(z, axis=0), wo_ref[...])


def _rwkv_post(y, r, k, v, g, x, ln_w, ln_b, r_k, w_o, e_red, e_exp):
    t, d = x.shape
    tm = TOKEN_TILE
    row = pl.BlockSpec((tm, d), lambda i: (i, 0))
    return pl.pallas_call(
        _rwkv_post_kernel,
        grid=(t // tm,),
        in_specs=[row] * 6 + [_full(ln_w.shape), _full(ln_b.shape), _full(r_k.shape),
                              _resident(w_o.shape), _full(e_red.shape), _full(e_exp.shape)],
        out_specs=row,
        out_shape=jax.ShapeDtypeStruct((t, d), F32),
        compiler_params=_params("parallel"),
        name="rwkv_post",
    )(y, r, k, v, g, x, ln_w, ln_b, r_k, w_o, e_red, e_exp)


def _store_dilated(dst_ref, y, stage_ref, dil):
    rows, d = y.shape
    if dil == 1:
        dst_ref[...] = y.astype(dst_ref.dtype)
        return
    n_tiles = d // LANES
    for c in range(n_tiles):
        stage_ref[c] = y[:, c * LANES:(c + 1) * LANES]
    for r in range(dil):
        for c in range(n_tiles):
            lo = r * d + c * LANES
            dst_ref[:, lo:lo + LANES] = (
                stage_ref[c, pl.ds(r, rows // dil, stride=dil), :].astype(dst_ref.dtype))


def _proj_blocks(hn, w_ref, n_normed, gain_ref, e_red, e_exp, emit):
    d = hn.shape[1]
    n_blk = w_ref.shape[1] // d
    normed = sorted(range(n_normed), reverse=True)
    order = normed + sorted(range(n_normed, n_blk), reverse=True)
    ys, ms = {}, {}

    def project(b):
        ys[b] = jnp.dot(hn, w_ref[:, b * d:(b + 1) * d].astype(BF16),
                        preferred_element_type=F32)

    def head_sums(b):
        if b in normed:
            ms[b] = _head_sum(ys[b] * ys[b], e_red) * (1.0 / HEAD_DIM)

    def finish(b):
        y = ys.pop(b)
        if b in normed:
            y = y * _spread(lax.rsqrt(ms.pop(b) + RMS_EPS), e_exp) * gain_ref[b]
        emit(b, y)

    for i in range(n_blk + 2):
        for lag, stage in enumerate((project, head_sums, finish)):
            if 0 <= i - lag < n_blk:
                stage(order[i - lag])


def _kv_proj_kernel(x_ref, gn_ref, w_ref, gain_ref, ered_ref, eexp_ref, o0_ref, o1_ref, o2_ref,
                    *stage_refs):
    n_groups = len(DIL_GROUPS)
    out_refs = (o0_ref, o1_ref, o2_ref)
    hn = _rms(x_ref[...], gn_ref[...]).astype(BF16)

    def emit(b, y):
        is_value, grp = divmod(b, n_groups)
        _store_dilated(out_refs[grp].at[is_value], y, stage_refs[grp], DIL_GROUPS[grp][1])

    _proj_blocks(hn, w_ref, n_groups, gain_ref, ered_ref[...], eexp_ref[...], emit)


def _kv_proj(x, gn, w_kv, gains, e_red, e_exp):
    t, d = x.shape
    tm = TOKEN_TILE
    return pl.pallas_call(
        _kv_proj_kernel,
        grid=(t // tm,),
        in_specs=[pl.BlockSpec((tm, d), lambda i: (i, 0)), _full(gn.shape),
                  _resident(w_kv.shape), _full(gains.shape), _full(e_red.shape),
                  _full(e_exp.shape)],
        out_specs=[pl.BlockSpec((2, tm // dil, dil * d), lambda i: (0, i, 0))
                   for _, dil in DIL_GROUPS],
        out_shape=[jax.ShapeDtypeStruct((2, t // dil, dil * d), BF16) for _, dil in DIL_GROUPS],
        scratch_shapes=[pltpu.VMEM((d // LANES, tm, LANES), F32) for _ in DIL_GROUPS],
        compiler_params=_params("parallel"),
        name="kv_proj",
    )(x, gn, w_kv, gains, e_red, e_exp)


def _q_proj_kernel(x_ref, gn_ref, w_ref, gain_ref, ered_ref, eexp_ref, o0_ref, o1_ref, o2_ref,
                   *stage_refs):
    out_refs = (o0_ref, o1_ref, o2_ref)
    hn = _rms(x_ref[...], gn_ref[...]).astype(BF16)
    scale = HEAD_DIM ** -0.5 * LOG2_E

    def emit(grp, y):
        _store_dilated(out_refs[grp], y * scale, stage_refs[grp], DIL_GROUPS[grp][1])

    _proj_blocks(hn, w_ref, len(DIL_GROUPS), gain_ref, ered_ref[...], eexp_ref[...], emit)


def _q_proj(x, gn, w_q, layer, gains, e_red, e_exp):
    t, d = x.shape
    tm = TOKEN_TILE
    return pl.pallas_call(
        _q_proj_kernel,
        grid=(t // tm,),
        in_specs=[pl.BlockSpec((tm, d), lambda i: (i, 0)), _full(gn.shape),
                  pl.BlockSpec((None,) + w_q.shape[1:], lambda i: (layer, 0, 0),
                               pipeline_mode=pl.Buffered(1)),
                  _full(gains.shape), _full(e_red.shape), _full(e_exp.shape)],
        out_specs=[pl.BlockSpec((tm // dil, dil * d), lambda i: (i, 0)) for _, dil in DIL_GROUPS],
        out_shape=[jax.ShapeDtypeStruct((t // dil, dil * d), BF16) for _, dil in DIL_GROUPS],
        scratch_shapes=[pltpu.VMEM((d // LANES, tm, LANES), F32) for _ in DIL_GROUPS],
        compiler_params=_params("parallel"),
        name="q_proj",
    )(x, gn, w_q, gains, e_red, e_exp)


def _dil_attn_kernel(q_ref, kp_ref, kc_ref, vp_ref, vc_ref, o_ref, m_ref, l_ref):
    n = pl.program_id(1)
    blk = ATT_BLK
    n_pairs = q_ref.shape[1] // LANES
    n_sub = q_ref.shape[0] // blk
    qi = lax.broadcasted_iota(jnp.int32, (blk, 2 * blk), 0)
    kj = lax.broadcasted_iota(jnp.int32, (blk, 2 * blk), 1)
    band = (kj >= qi) & (kj <= qi + blk)
    valid = [band & ((n > 0) | (kj >= blk))] + [band] * (n_sub - 1)
    lane = lax.broadcasted_iota(jnp.int32, (blk, LANES), 1)
    head0 = lane < HEAD_DIM
    dn_nt = (((1,), (1,)), ((), ()))
    zero = jnp.zeros((blk, LANES), BF16)
    lanes_of = lambda a, p: a[:, p * LANES:(p + 1) * LANES]

    def keys_of(prev_ref, cur_ref, s, p):
        if s == 0:
            return jnp.concatenate([lanes_of(prev_ref, p), lanes_of(cur_ref, p)[:blk]], axis=0)
        return lanes_of(cur_ref, p)[(s - 1) * blk:(s + 1) * blk]

    waves = [(s, range(p0, p0 + ATT_WAVE)) for s in range(n_sub)
             for p0 in range(0, n_pairs, ATT_WAVE)]
    scores, probs = {}, {}
    m_acc = [jnp.zeros((blk, LANES), F32) for _ in range(n_sub)]
    l_acc = [jnp.ones((blk, LANES), F32) for _ in range(n_sub)]

    def score_stage(s, ps):
        for p in ps:
            qp = lanes_of(q_ref, p)[s * blk:(s + 1) * blk]
            q2 = jnp.concatenate([jnp.where(head0, qp, zero), jnp.where(head0, zero, qp)], axis=0)
            scores[s, p] = lax.dot_general(q2, keys_of(kp_ref, kc_ref, s, p), dn_nt,
                                           preferred_element_type=F32)

    def softmax_stage(s, ps):
        for p in ps:
            sc2 = scores.pop((s, p))
            halves = []
            for hh in range(2):
                sc = jnp.where(valid[s], sc2[hh * blk:(hh + 1) * blk], NEG_INF)
                mx = jnp.max(sc, axis=-1, keepdims=True)
                pe = jnp.exp2(sc - mx)
                hit = lane == 2 * p + hh
                m_acc[s] = jnp.where(hit, mx, m_acc[s])
                l_acc[s] = jnp.where(hit, jnp.sum(pe, axis=-1, keepdims=True), l_acc[s])
                halves.append(pe.astype(BF16))
            probs[s, p] = jnp.concatenate(halves, axis=0)

    def output_stage(s, ps):
        for p in ps:
            o2 = jnp.dot(probs.pop((s, p)), keys_of(vp_ref, vc_ref, s, p),
                         preferred_element_type=F32)
            o_ref[s * blk:(s + 1) * blk, p * LANES:(p + 1) * LANES] = jnp.where(
                head0, o2[:blk], o2[blk:]).astype(o_ref.dtype)

    for i in range(len(waves) + 2):
        for lag, stage in enumerate((score_stage, softmax_stage, output_stage)):
            if 0 <= i - lag < len(waves):
                stage(*waves[i - lag])
    for s in range(n_sub):
        m_ref[s * blk:(s + 1) * blk, :] = m_acc[s]
        l_ref[s * blk:(s + 1) * blk, :] = l_acc[s]


def _dil_attn(q, kv, dil):
    m_rows, wide = q.shape
    d = wide // dil
    blk = ATT_BLK
    step = ATT_SUB * blk
    cur = lambda a: pl.BlockSpec((None, step, d), lambda r, n: (a, n, r))
    prev = lambda a: pl.BlockSpec((None, blk, d),
                                  lambda r, n: (a, jnp.maximum(ATT_SUB * n - 1, 0), r))
    return pl.pallas_call(
        _dil_attn_kernel,
        grid=(dil, m_rows // step),
        in_specs=[pl.BlockSpec((step, d), lambda r, n: (n, r)), prev(0), cur(0), prev(1), cur(1)],
        out_specs=[pl.BlockSpec((step, d), lambda r, n: (n, r)),
                   pl.BlockSpec((step, LANES), lambda r, n: (n, r)),
                   pl.BlockSpec((step, LANES), lambda r, n: (n, r))],
        out_shape=[jax.ShapeDtypeStruct((m_rows, dil * d), BF16),
                   jax.ShapeDtypeStruct((m_rows, dil * LANES), F32),
                   jax.ShapeDtypeStruct((m_rows, dil * LANES), F32)],
        compiler_params=_params("parallel", "parallel"),
        name="dil_attn",
    )(q, kv, kv, kv, kv)


def _load_dilated(src_ref, stage_ref, dil):
    if dil == 1:
        return src_ref[...]
    sub = src_ref.shape[0]
    n_tiles = src_ref.shape[1] // dil // LANES
    for r in range(dil):
        for c in range(n_tiles):
            lo = (r * n_tiles + c) * LANES
            stage_ref[c, pl.ds(r, sub, stride=dil), :] = src_ref[:, lo:lo + LANES].astype(F32)
    return jnp.concatenate([stage_ref[c] for c in range(n_tiles)], axis=1)


def _attn_combine_kernel(o0, o1, o2, m0, m1, m2, l0, l1, l2, x_ref, wo_ref, eexp_ref, out_ref,
                         os1, os2, ms1, ms2, ls1, ls2):
    dils = [dil for _, dil in DIL_GROUPS]
    ms = [_load_dilated(ref, st, dil) for ref, st, dil in zip((m0, m1, m2), (None, ms1, ms2), dils)]
    ls = [_load_dilated(ref, st, dil) for ref, st, dil in zip((l0, l1, l2), (None, ls1, ls2), dils)]
    top = jnp.maximum(jnp.maximum(ms[0], ms[1]), ms[2])
    es = [jnp.exp2(mg - top) for mg in ms]
    den = es[0] * ls[0] + es[1] * ls[1] + es[2] * ls[2]
    e_exp = eexp_ref[...]
    acc = None
    for eg, ref, st, dil in zip(es, (o0, o1, o2), (None, os1, os2), dils):
        term = _spread(eg / den, e_exp) * _load_dilated(ref, st, dil)
        acc = term if acc is None else acc + term
    out_ref[...] = x_ref[...] + _mm(acc, wo_ref[...])


def _attn_combine(parts, x, w_o, e_exp):
    t, d = x.shape
    tm = TOKEN_TILE
    row = pl.BlockSpec((tm, d), lambda i: (i, 0))
    dils = [dil for _, dil in DIL_GROUPS]
    o_specs = [pl.BlockSpec((tm // dil, dil * d), lambda i: (i, 0)) for dil in dils]
    s_specs = [pl.BlockSpec((tm // dil, dil * LANES), lambda i: (i, 0)) for dil in dils]
    os_, ms_, ls_ = zip(*parts)
    return pl.pallas_call(
        _attn_combine_kernel,
        grid=(t // tm,),
        in_specs=o_specs + s_specs + s_specs + [row, _resident(w_o.shape), _full(e_exp.shape)],
        out_specs=row,
        out_shape=jax.ShapeDtypeStruct((t, d), F32),
        scratch_shapes=([pltpu.VMEM((d // LANES, tm, LANES), F32)] * 2
                        + [pltpu.VMEM((1, tm, LANES), F32)] * 4),
        compiler_params=_params("parallel"),
        name="attn_combine",
    )(*os_, *ms_, *ls_, x, w_o, e_exp)


def kernel(x, ffn_norm, ffn_w_in, ffn_w_out, mix_norm, rwkv_mu, rwkv_w_rkv, rwkv_w0, rwkv_w1, rwkv_w2, rwkv_a0, rwkv_a1, rwkv_a2, rwkv_v0, rwkv_v1, rwkv_v2, rwkv_g1, rwkv_g2, rwkv_k_k, rwkv_k_a, rwkv_r_k, rwkv_ln_w, rwkv_ln_b, rwkv_w_o, kv_norm, w_kv, k_norm, attn_w_q, q_norm, attn_w_o):
    bsz, t, d = x.shape
    assert bsz == 1 and d % LANES == 0 and t % max(w for w, _ in DIL_GROUPS) == 0
    assert all(w // dil == ATT_BLK for w, dil in DIL_GROUPS)
    depth = ffn_norm.shape[0]
    n_a = rwkv_mu.shape[0]
    n_groups = len(DIL_GROUPS)
    row = lambda p: p.reshape(1, -1)

    ffn_g = ffn_norm.reshape(2 * depth, 1, d)
    ffn_wi = ffn_w_in.reshape(2 * depth, d, -1)
    ffn_wo = ffn_w_out.reshape(2 * depth, -1, d)
    head_of_lane = jnp.arange(d) // HEAD_DIM
    one_hot = (jnp.arange(LANES)[:, None] == head_of_lane[None, :]).astype(BF16)
    e_red = one_hot.T
    e_exp = jnp.concatenate([one_hot, one_hot], axis=0)

    xs = x.reshape(t, d)
    v_first = None
    kvs = None
    for l in range(depth):
        xs = _ffn(xs, ffn_g, ffn_wi, ffn_wo, 2 * l)
        if l < n_a:
            vres = None
            if l > 0:
                vres = (row(rwkv_v0[l - 1]), rwkv_v1[l - 1].astype(BF16),
                        rwkv_v2[l - 1].astype(BF16), v_first)
            r, ld, k, v, na, nb, g = _rwkv_proj(
                xs, row(mix_norm[l]), rwkv_mu[l], rwkv_w_rkv, l, row(rwkv_w0[l]),
                rwkv_w1[l].astype(BF16), rwkv_w2[l].astype(BF16), row(rwkv_a0[l]),
                rwkv_a1[l].astype(BF16), rwkv_a2[l].astype(BF16), rwkv_g1[l].astype(BF16),
                rwkv_g2[l].astype(BF16), row(rwkv_k_k[l]), row(rwkv_k_a[l]), e_red, e_exp, vres)
            if l == 0:
                v_first = v
            y = _rwkv_scan(r, ld, k, v, na, nb)
            xs = _rwkv_post(y, r, k, v, g, xs, row(rwkv_ln_w[l]), row(rwkv_ln_b[l]),
                            row(rwkv_r_k[l]), rwkv_w_o[l].astype(BF16), e_red, e_exp)
        else:
            i = l - n_a
            q_gain = jnp.tile(q_norm[i], (1, d // HEAD_DIM)).reshape(n_groups, 1, d)
            qs = _q_proj(xs, row(mix_norm[l]), attn_w_q, i, q_gain, e_red, e_exp)
            parts = [_dil_attn(qs[grp], kvs[grp], DIL_GROUPS[grp][1]) for grp in range(n_groups)]
            xs = _attn_combine(parts, xs, attn_w_o[i].astype(BF16), e_exp)
        xs = _ffn(xs, ffn_g, ffn_wi, ffn_wo, 2 * l + 1)
        if l == n_a - 1:
            k_gain = jnp.tile(k_norm, (1, d // HEAD_DIM)).reshape(n_groups, 1, d)
            kvs = _kv_proj(xs, row(kv_norm), w_kv.astype(BF16), k_gain, e_red, e_exp)
    return xs.reshape(bsz, t, d)
```
